```python
import jax
import jax.numpy as jnp
from jax import lax
import numpy as np

D_MODEL = 1024
BATCH = 32
SEQ = 2048
DEPTH = 1
DEC_BATCH = 128
DEC_SEQ = 4
PAST_LEN = 16384
PAGE_SIZE = 128

MLA_HEADS = 8
MLA_NOPE = 64
MLA_ROPE = 32
MLA_V = 64
Q_LORA = 384
KV_LORA = 256
MLA_ROW = KV_LORA + MLA_ROPE
FOX_HEADS = 8
FOX_KV_HEADS = 4
FOX_GROUP = FOX_HEADS // FOX_KV_HEADS
FOX_DIM = 64
MEM_LEN = 256
MEM_HEADS = 4
MEM_DIM = 64
D_FF = 4 * D_MODEL

MIX_WIDTH = MLA_HEADS * MLA_V + FOX_HEADS * FOX_DIM
IN_WIDTHS = (Q_LORA, KV_LORA, MLA_ROPE, FOX_HEADS * FOX_DIM, FOX_KV_HEADS * FOX_DIM, FOX_KV_HEADS * FOX_DIM, FOX_HEADS)
D_IN = sum(IN_WIDTHS)
IN_SPLITS = tuple(int(s) for s in np.cumsum(IN_WIDTHS)[:-1])

ROPE_BASE = 10000.0
Q_BLOCK = 128
EPS = 1e-6
NEG = -1e30
MLA_SCALE = (MLA_NOPE + MLA_ROPE) ** -0.5
FOX_SCALE = FOX_DIM ** -0.5
MEM_SCALE = MEM_DIM ** -0.5

kernel_name = 'hybrid_mla_fox_memory_decoder_step'


def rmsnorm(x, g):
    xf = x.astype(jnp.float32)
    y = xf * lax.rsqrt(jnp.mean(xf * xf, axis=-1, keepdims=True) + EPS)
    return (y * g.astype(jnp.float32)).astype(x.dtype)


def rope(x, pos):
    half = MLA_ROPE // 2
    inv_freq = ROPE_BASE ** (-jnp.arange(half, dtype=jnp.float32) / half)
    ang = pos.astype(jnp.float32)[:, None] * inv_freq[None, :]
    cos = jnp.cos(ang)[:, None, :]
    sin = jnp.sin(ang)[:, None, :]
    xf = x.astype(jnp.float32)
    x1, x2 = xf[..., :half], xf[..., half:]
    return jnp.concatenate([x1 * cos - x2 * sin, x1 * sin + x2 * cos], axis=-1).astype(x.dtype)


def mixer_inputs(xn, pos, w):
    B, S, _ = xn.shape
    c_q, c_kv, k_r, f_q, f_k, f_v, f_l = jnp.split(xn @ w['w_in'], IN_SPLITS, axis=-1)
    q = (rmsnorm(c_q, w['g_q_lat']) @ w['w_q_up']).reshape(B, S, MLA_HEADS, MLA_NOPE + MLA_ROPE)
    q_nope = rmsnorm(q[..., :MLA_NOPE], w['g_q_nope'])
    q_rope = rope(rmsnorm(q[..., MLA_NOPE:], w['g_q_rope']), pos)
    mla_q = jnp.concatenate([q_nope, q_rope], axis=-1) * MLA_SCALE
    k_rope = rope(rmsnorm(k_r, w['g_k_rope'])[:, :, None, :], pos)[:, :, 0, :]
    mla_row = jnp.concatenate([rmsnorm(c_kv, w['g_kv_lat']), k_rope], axis=-1)
    fox_q = rmsnorm(f_q.reshape(B, S, FOX_KV_HEADS, FOX_GROUP, FOX_DIM), w['g_fox_q']) * FOX_SCALE
    fox_k = rmsnorm(f_k.reshape(B, S, FOX_KV_HEADS, FOX_DIM), w['g_fox_k'])
    fox_v = f_v.reshape(B, S, FOX_KV_HEADS, FOX_DIM)
    log_f = jax.nn.log_sigmoid(f_l.astype(jnp.float32) + w['b_forget'].astype(jnp.float32))
    return mla_q[:, :, :, None, :], mla_row, fox_q, fox_k, fox_v, log_f


def mla_kv(rows, w):
    B, S, _ = rows.shape
    kv = (rows[..., :KV_LORA] @ w['w_kv_up']).reshape(B, S, MLA_HEADS, MLA_NOPE + MLA_V)
    k_nope = rmsnorm(kv[..., :MLA_NOPE], w['g_k_nope'])
    k_rope = jnp.broadcast_to(rows[:, :, None, KV_LORA:], (B, S, MLA_HEADS, MLA_ROPE))
    return jnp.concatenate([k_nope, k_rope], axis=-1), kv[..., MLA_NOPE:]


def causal_block_attention(q, k, v, cum):
    B, S, Hk, G, _ = q.shape
    E = v.shape[-1]
    k_pos = jnp.arange(S)

    def one_block(i):
        start = i * Q_BLOCK
        qb = lax.dynamic_slice_in_dim(q, start, Q_BLOCK, axis=1)
        s = jnp.einsum('bqhgd,bshd->bhgqs', qb, k).astype(jnp.float32)
        if cum is not None:
            cq = lax.dynamic_slice_in_dim(cum, start, Q_BLOCK, axis=3)
            s = s + (cq[..., :, None] - cum[..., None, :])
        q_pos = start + jnp.arange(Q_BLOCK)
        s = jnp.where(k_pos[None, :] <= q_pos[:, None], s, NEG)
        p = jax.nn.softmax(s, axis=-1)
        return jnp.einsum('bhgqs,bshe->bqhge', p.astype(v.dtype), v)

    out = lax.map(one_block, jnp.arange(S // Q_BLOCK))
    return jnp.moveaxis(out, 0, 1).reshape(B, S, Hk * G * E)


def partial_attention(q, k, v, bias, mask):
    s = jnp.einsum('bqhgd,bshd->bhgqs', q, k).astype(jnp.float32)
    if bias is not None:
        s = s + bias
    if mask is not None:
        s = jnp.where(mask, s, NEG)
    m = jnp.max(s, axis=-1)
    p = jnp.exp(s - m[..., None])
    o = jnp.einsum('bhgqs,bshe->bhgqe', p, v.astype(jnp.float32))
    return m, jnp.sum(p, axis=-1), o


def merge_partials(paged, local):
    m, l, o = [jnp.concatenate([a, b[None]], axis=0) for a, b in zip(paged, local)]
    m_max = jnp.max(m, axis=0)
    wgt = jnp.exp(m - m_max)
    out = jnp.sum(wgt[..., None] * o, axis=0) / jnp.sum(wgt * l, axis=0)[..., None]
    B, Hk, G, T, E = out.shape
    return jnp.transpose(out, (0, 3, 1, 2, 4)).reshape(B, T, Hk * G * E)


def memory_kv(mem, w):
    B, M, _ = mem.shape
    mn = rmsnorm(mem, w['g_mem'])
    k = rmsnorm((mn @ w['w_mk']).reshape(B, M, MEM_HEADS, MEM_DIM), w['g_mk'])
    v = (mn @ w['w_mv']).reshape(B, M, MEM_HEADS, MEM_DIM)
    return k, v


def finish_layer(x, o_mix, mem_k, mem_v, w):
    B, S, _ = x.shape
    h = x + o_mix.astype(x.dtype) @ w['w_o']
    hn = rmsnorm(h, w['g_cross'])
    q = rmsnorm((hn @ w['w_mq']).reshape(B, S, MEM_HEADS, MEM_DIM), w['g_mq']) * MEM_SCALE
    p = jax.nn.softmax(jnp.einsum('bqhd,bkhd->bhqk', q, mem_k).astype(jnp.float32), axis=-1)
    o = jnp.einsum('bhqk,bkhd->bqhd', p.astype(mem_v.dtype), mem_v).reshape(B, S, MEM_HEADS * MEM_DIM)
    h = h + o @ w['w_mo']
    hn = rmsnorm(h, w['g_ffn'])
    return h + jnp.square(jax.nn.relu(hn @ w['w_up'])) @ w['w_down']


def prompt_layer(x, mem, w):
    B, S, _ = x.shape
    pos = jnp.arange(S, dtype=jnp.int32)
    mla_q, mla_row, fox_q, fox_k, fox_v, log_f = mixer_inputs(rmsnorm(x, w['g_mix']), pos, w)
    k, v = mla_kv(mla_row, w)
    o_mla = causal_block_attention(mla_q, k, v, None)
    cum = jnp.moveaxis(jnp.cumsum(log_f, axis=1).reshape(B, S, FOX_KV_HEADS, FOX_GROUP), 1, -1)
    o_fox = causal_block_attention(fox_q, fox_k, fox_v, cum)
    mem_k, mem_v = memory_kv(mem, w)
    y = finish_layer(x, jnp.concatenate([o_mla, o_fox], axis=-1), mem_k, mem_v, w)
    return y, (mla_row, fox_k, fox_v, log_f, mem_k, mem_v)


def sample_layer(x, layer, cache_mla, cache_fox_k, cache_fox_v, cache_fox_logf, cache_mem_k, cache_mem_v, page_table, w):
    B, T, _ = x.shape
    n_pages = page_table.shape[1]
    page = cache_mla.shape[2]
    past = n_pages * page
    pos = past + jnp.arange(T, dtype=jnp.int32)
    mla_q, mla_row, fox_q, fox_k, fox_v, log_f = mixer_inputs(rmsnorm(x, w['g_mix']), pos, w)
    causal = jnp.tril(jnp.ones((T, T), dtype=bool))
    pages_by_step = page_table.T

    def mla_page(pages):
        k, v = mla_kv(cache_mla[layer, pages], w)
        return partial_attention(mla_q, k, v, None, None)

    k_new, v_new = mla_kv(mla_row, w)
    o_mla = merge_partials(lax.map(mla_page, pages_by_step),
                           partial_attention(mla_q, k_new, v_new, None, causal))

    log_f_past = cache_fox_logf[layer, page_table].reshape(B, past, FOX_HEADS).astype(jnp.float32)
    suffix = lax.cumsum(log_f_past, axis=1, reverse=True) - log_f_past
    suffix = jnp.transpose(suffix.reshape(B, n_pages, page, FOX_KV_HEADS, FOX_GROUP), (1, 0, 3, 4, 2))
    cum = jnp.moveaxis(jnp.cumsum(log_f, axis=1).reshape(B, T, FOX_KV_HEADS, FOX_GROUP), 1, -1)

    def fox_page(args):
        pages, suf = args
        bias = cum[..., :, None] + suf[..., None, :]
        return partial_attention(fox_q, cache_fox_k[layer, pages], cache_fox_v[layer, pages], bias, None)

    local_bias = cum[..., :, None] - cum[..., None, :]
    o_fox = merge_partials(lax.map(fox_page, (pages_by_step, suffix)),
                           partial_attention(fox_q, fox_k, fox_v, local_bias, causal))

    y = finish_layer(x, jnp.concatenate([o_mla, o_fox], axis=-1), cache_mem_k[layer], cache_mem_v[layer], w)
    return y, (mla_row, fox_k, fox_v, log_f)


def setup_inputs(seed: int = 0) -> dict:
    key = jax.random.key(seed)
    keys = iter(jax.random.split(key, 48))

    def normal(shape, scale=1.0):
        return jax.random.normal(next(keys), shape, dtype=jnp.float32) * scale

    def gain(n):
        return 1.0 + 0.02 * normal((DEPTH, n))

    L = DEPTH
    n_pages = PAST_LEN // PAGE_SIZE
    n_used = DEC_BATCH * n_pages
    n_pool = n_used + (n_used + 3) // 4
    page_table = jax.random.permutation(next(keys), n_pool)[:n_used].astype(jnp.int32).reshape(DEC_BATCH, n_pages)
    return {
        'x_prompt': normal((BATCH, SEQ, D_MODEL)),
        'x_sample': normal((DEC_BATCH, DEC_SEQ, D_MODEL)),
        'cache_mla': normal((L, n_pool, PAGE_SIZE, MLA_ROW)),
        'cache_fox_k': normal((L, n_pool, PAGE_SIZE, FOX_KV_HEADS, FOX_DIM)),
        'cache_fox_v': normal((L, n_pool, PAGE_SIZE, FOX_KV_HEADS, FOX_DIM)),
        'cache_fox_logf': jax.nn.log_sigmoid(4.0 + normal((L, n_pool, PAGE_SIZE, FOX_HEADS))),
        'cache_mem_k': normal((L, DEC_BATCH, MEM_LEN, MEM_HEADS, MEM_DIM)),
        'cache_mem_v': normal((L, DEC_BATCH, MEM_LEN, MEM_HEADS, MEM_DIM)),
        'page_table': page_table,
        'mem_prompt': normal((BATCH, MEM_LEN, D_MODEL)),
        'g_mix': gain(D_MODEL),
        'w_in': normal((L, D_MODEL, D_IN), D_MODEL ** -0.5),
        'g_q_lat': gain(Q_LORA),
        'w_q_up': normal((L, Q_LORA, MLA_HEADS * (MLA_NOPE + MLA_ROPE)), Q_LORA ** -0.5),
        'g_kv_lat': gain(KV_LORA),
        'g_k_rope': gain(MLA_ROPE),
        'g_q_nope': gain(MLA_NOPE),
        'g_q_rope': gain(MLA_ROPE),
        'g_k_nope': gain(MLA_NOPE),
        'w_kv_up': normal((L, KV_LORA, MLA_HEADS * (MLA_NOPE + MLA_V)), KV_LORA ** -0.5),
        'g_fox_q': gain(FOX_DIM),
        'g_fox_k': gain(FOX_DIM),
        'b_forget': jax.random.uniform(next(keys), (L, FOX_HEADS), jnp.float32, 2.0, 6.0),
        'w_o': normal((L, MIX_WIDTH, D_MODEL), MIX_WIDTH ** -0.5),
        'g_cross': gain(D_MODEL),
        'g_mem': gain(D_MODEL),
        'w_mq': normal((L, D_MODEL, MEM_HEADS * MEM_DIM), D_MODEL ** -0.5),
        'w_mk': normal((L, D_MODEL, MEM_HEADS * MEM_DIM), D_MODEL ** -0.5),
        'w_mv': normal((L, D_MODEL, MEM_HEADS * MEM_DIM), D_MODEL ** -0.5),
        'g_mq': gain(MEM_DIM),
        'g_mk': gain(MEM_DIM),
        'w_mo': normal((L, MEM_HEADS * MEM_DIM, D_MODEL), (MEM_HEADS * MEM_DIM) ** -0.5),
        'g_ffn': gain(D_MODEL),
        'w_up': normal((L, D_MODEL, D_FF), D_MODEL ** -0.5),
        'w_down': normal((L, D_FF, D_MODEL), D_FF ** -0.5),
    }


def reference(x_prompt, x_sample, cache_mla, cache_fox_k, cache_fox_v, cache_fox_logf, cache_mem_k, cache_mem_v,
              page_table, mem_prompt, g_mix, w_in, g_q_lat, w_q_up, g_kv_lat, g_k_rope, g_q_nope, g_q_rope,
              g_k_nope, w_kv_up, g_fox_q, g_fox_k, b_forget, w_o, g_cross, g_mem, w_mq, w_mk, w_mv, g_mq, g_mk,
              w_mo, g_ffn, w_up, w_down):
    weights = dict(g_mix=g_mix, w_in=w_in, g_q_lat=g_q_lat, w_q_up=w_q_up, g_kv_lat=g_kv_lat, g_k_rope=g_k_rope,
                   g_q_nope=g_q_nope, g_q_rope=g_q_rope, g_k_nope=g_k_nope, w_kv_up=w_kv_up, g_fox_q=g_fox_q,
                   g_fox_k=g_fox_k, b_forget=b_forget, w_o=w_o, g_cross=g_cross, g_mem=g_mem, w_mq=w_mq,
                   w_mk=w_mk, w_mv=w_mv, g_mq=g_mq, g_mk=g_mk, w_mo=w_mo, g_ffn=g_ffn, w_up=w_up, w_down=w_down)
    y_prompt, y_sample = x_prompt, x_sample
    prompt_states, sample_states = [], []
    for layer in range(DEPTH):
        w = {name: arr[layer] for name, arr in weights.items()}
        y_prompt, st_p = prompt_layer(y_prompt, mem_prompt, w)
        y_sample, st_s = sample_layer(y_sample, layer, cache_mla, cache_fox_k, cache_fox_v, cache_fox_logf,
                                      cache_mem_k, cache_mem_v, page_table, w)
        prompt_states.append(st_p)
        sample_states.append(st_s)
    p_mla, p_fox_k, p_fox_v, p_fox_logf, p_mem_k, p_mem_v = [jnp.stack(z) for z in zip(*prompt_states)]
    s_mla, s_fox_k, s_fox_v, s_fox_logf = [jnp.stack(z) for z in zip(*sample_states)]
    return (y_prompt, y_sample, p_mla, p_fox_k, p_fox_v, p_fox_logf, p_mem_k, p_mem_v,
            s_mla, s_fox_k, s_fox_v, s_fox_logf)
```

```python
import functools

import numpy as np
import jax
import jax.numpy as jnp
from jax import lax
from jax.experimental import pallas as pl
from jax.experimental.pallas import tpu as pltpu

F32 = jnp.float32
BF16 = jnp.bfloat16

D_MODEL = 1024
Q_LORA = 384
KV_LORA = 256
ROPE = 32
HALF = ROPE // 2
NOPE = 64
MLA_V = 64
MLA_HEADS = 8
MLA_ROW = KV_LORA + ROPE
FOX_HEADS = 8
FOX_KV = 4
FOX_DIM = 64
MEM_HEADS = 4
MEM_DIM = 64
MEM_W = MEM_HEADS * MEM_DIM
D_FF = 4096
IN_SPLITS = (0, 384, 640, 672, 1184, 1440, 1696, 1704)
ROPE_BASE = 10000.0
EPS = 1e-6
NEG = -1e30
MLA_SCALE = (NOPE + ROPE) ** -0.5
FOX_SCALE = FOX_DIM ** -0.5
MEM_SCALE = MEM_DIM ** -0.5

LANE = 128
SLAB = 2 * LANE
VMEM_LIMIT = 56 * 1024 * 1024
PROJ_TM = 256
ATTN_T = 256
FIN_TM = 512
DEC_T = 8
PPS = 8
FFN_CHUNK = 1024

C_CQ, C_CKV, C_KR, C_KROT, C_FQ, C_FK, C_FV, C_FL, W_IN_EXP = 0, 384, 640, 768, 896, 1920, 2176, 2432, 2560
G_QN_E, G_QN_O, G_QR, G_QR_ROT, G_KR, G_KR_ROT, G_KN, G_FQ_E, G_FQ_O, G_FK, G_BF, G_KAUG = range(12)


def _const_spec(shape):
    n = len(shape)
    return pl.BlockSpec(shape, lambda *_: (0,) * n, pipeline_mode=pl.Buffered(1))


def _rms(x, width):
    return x * lax.rsqrt(jnp.sum(x * x, axis=-1, keepdims=True) * (1.0 / width) + EPS)


def _group64_rms(x):
    out = []
    lane = lax.broadcasted_iota(jnp.int32, (1, LANE), 1)
    lo = lane < 64
    for j in range(x.shape[1] // LANE):
        s = x[:, j * LANE:(j + 1) * LANE]
        s2 = s * s
        r_lo = lax.rsqrt(jnp.sum(jnp.where(lo, s2, 0.0), axis=-1, keepdims=True) * (1.0 / 64) + EPS)
        r_hi = lax.rsqrt(jnp.sum(jnp.where(lo, 0.0, s2), axis=-1, keepdims=True) * (1.0 / 64) + EPS)
        out.append(s * jnp.where(lo, r_lo, r_hi))
    return out[0] if len(out) == 1 else jnp.concatenate(out, axis=1)


def _split3(x):
    hi = x.astype(BF16)
    r1 = x - hi.astype(F32)
    mid = r1.astype(BF16)
    lo = (r1 - mid.astype(F32)).astype(BF16)
    return hi, mid, lo


def _dot(a, b):
    return jnp.dot(a, b, preferred_element_type=F32)


def _dot_nt(a, b):
    return lax.dot_general(a, b, (((1,), (1,)), ((), ())), preferred_element_type=F32)


def _dot3(x, w):
    hi, mid, lo = _split3(x)
    return _dot(hi, w) + _dot(mid, w) + _dot(lo, w)


def _proj_kernel(x_ref, cos_ref, sin_ref, tri_ref, gmix_ref, win_ref, gql_ref, wq_ref, gkv_ref, wkv_ref,
                 g128_ref, place_ref, baseq_ref,
                 row_ref, qm_ref, kn_ref, vm_ref, kr_ref, fq_ref, fk_ref, fkb_ref, fv_ref, fvb_ref,
                 lf_ref, cum_ref, kaug_ref, carry_ref, *, tiles_per_seq):
    i = pl.program_id(0)
    cos = cos_ref[...]
    sin = sin_ref[...]

    def g(r):
        return g128_ref[r:r + 1, :]

    xn = (_rms(x_ref[...], D_MODEL) * gmix_ref[...]).astype(BF16)
    a = _dot(xn, win_ref[...])

    cq = (_rms(a[:, C_CQ:C_CQ + Q_LORA], Q_LORA) * gql_ref[...]).astype(BF16)
    q = _dot(cq, wq_ref[...])
    for h in range(MLA_HEADS):
        nope = q[:, h * SLAB:h * SLAB + LANE]
        rope = q[:, h * SLAB + LANE:(h + 1) * SLAB]
        rot = q[:, MLA_HEADS * SLAB + h * LANE:MLA_HEADS * SLAB + (h + 1) * LANE]
        rn = lax.rsqrt(jnp.sum(nope * nope, axis=-1, keepdims=True) * (1.0 / NOPE) + EPS)
        rr = lax.rsqrt(jnp.sum(rope * rope, axis=-1, keepdims=True) * (1.0 / ROPE) + EPS)
        qm_ref[:, h * SLAB:h * SLAB + LANE] = (nope * rn * g(G_QN_O if h % 2 else G_QN_E)).astype(BF16)
        qm_ref[:, h * SLAB + LANE:(h + 1) * SLAB] = (
            rr * (rope * g(G_QR) * cos + rot * g(G_QR_ROT) * sin)).astype(BF16)

    lat = _rms(a[:, C_CKV:C_CKV + KV_LORA], KV_LORA) * gkv_ref[...]
    kr = a[:, C_KR:C_KR + LANE]
    krot = a[:, C_KROT:C_KROT + LANE]
    rk = lax.rsqrt(jnp.sum(kr * kr, axis=-1, keepdims=True) * (1.0 / ROPE) + EPS)
    k_rope = rk * (kr * g(G_KR) * cos + krot * g(G_KR_ROT) * sin)
    row_ref[:, 0:KV_LORA] = lat
    row_ref[:, KV_LORA:MLA_ROW] = k_rope[:, 0:ROPE]
    kr_ref[...] = k_rope.astype(BF16)

    kv = _dot(lat.astype(BF16), wkv_ref[...])
    gkn = g(G_KN)
    kn = _group64_rms(kv[:, 0:MLA_HEADS * NOPE])
    kn_ref[...] = (kn * jnp.concatenate([gkn] * 4, axis=1)).astype(BF16)
    vm_ref[...] = kv[:, MLA_HEADS * NOPE:].astype(BF16)

    for h in range(FOX_HEADS):
        s = a[:, C_FQ + h * LANE:C_FQ + (h + 1) * LANE]
        r = lax.rsqrt(jnp.sum(s * s, axis=-1, keepdims=True) * (1.0 / FOX_DIM) + EPS)
        fq_ref[:, h * SLAB:h * SLAB + LANE] = (s * r * g(G_FQ_O if (h // 2) % 2 else G_FQ_E)).astype(BF16)
    fk = _group64_rms(a[:, C_FK:C_FK + 2 * LANE]) * jnp.concatenate([g(G_FK)] * 2, axis=1)
    fk_ref[...] = fk
    fkb_ref[...] = fk.astype(BF16)
    fv = a[:, C_FV:C_FV + 2 * LANE]
    fv_ref[...] = fv
    fvb_ref[...] = fv.astype(BF16)

    z = a[:, C_FL:C_FL + LANE] + g(G_BF)
    lane = lax.broadcasted_iota(jnp.int32, (1, LANE), 1)
    logf = jnp.where(lane < FOX_HEADS, jnp.minimum(z, 0.0) - jnp.log1p(jnp.exp(-jnp.abs(z))), 0.0)
    lf_ref[...] = logf[:, 0:FOX_HEADS]

    @pl.when(i % tiles_per_seq == 0)
    def _():
        carry_ref[...] = jnp.zeros_like(carry_ref)

    l_hi, l_mid, l_lo = _split3(logf)
    tri = tri_ref[...]
    cum = _dot(tri, l_hi) + _dot(tri, l_mid) + _dot(tri, l_lo) + carry_ref[0:1, :]
    cum_ref[...] = cum
    tm = cum.shape[0]
    carry_ref[...] = jnp.broadcast_to(cum[tm - 1:tm, :], carry_ref.shape)

    c_hi, c_mid, c_lo = _split3(cum)
    aug = _dot(c_hi, place_ref[0]) + _dot(c_mid, place_ref[1]) + _dot(c_lo, place_ref[2])
    kaug_ref[...] = (aug[:, 0:LANE] + g(G_KAUG)).astype(BF16)
    for h in range(FOX_HEADS):
        fq_ref[:, h * SLAB + LANE:(h + 1) * SLAB] = (
            aug[:, (h + 1) * LANE:(h + 2) * LANE] + baseq_ref[:, h * LANE:(h + 1) * LANE]).astype(BF16)


def _proj(x2, cos_t, sin_t, tri, w, tiles_per_seq):
    n = x2.shape[0]
    tm = PROJ_TM
    n_tab = cos_t.shape[0] // tm
    row = lambda width: pl.BlockSpec((tm, width), lambda i: (i, 0))
    tab = pl.BlockSpec((tm, LANE), lambda i: (i % n_tab, 0))
    out_shapes = [
        ((n, MLA_ROW), F32), ((n, MLA_HEADS * SLAB), BF16), ((n, MLA_HEADS * NOPE), BF16),
        ((n, MLA_HEADS * MLA_V), BF16), ((n, LANE), BF16), ((n, FOX_HEADS * SLAB), BF16),
        ((n, FOX_KV * FOX_DIM), F32), ((n, FOX_KV * FOX_DIM), BF16), ((n, FOX_KV * FOX_DIM), F32),
        ((n, FOX_KV * FOX_DIM), BF16), ((n, FOX_HEADS), F32), ((n, LANE), F32), ((n, LANE), BF16)]
    consts = [w['g_mix'], w['w_in'], w['g_q_lat'], w['w_q'], w['g_kv_lat'], w['w_kv'], w['g128'], w['place'],
              w['base_q']]
    return pl.pallas_call(
        functools.partial(_proj_kernel, tiles_per_seq=tiles_per_seq),
        grid=(n // tm,),
        in_specs=[row(D_MODEL), tab, tab, _const_spec(tri.shape)] + [_const_spec(c.shape) for c in consts],
        out_specs=[row(s[1]) for s, _ in out_shapes],
        out_shape=[jax.ShapeDtypeStruct(s, d) for s, d in out_shapes],
        scratch_shapes=[pltpu.VMEM((8, LANE), F32)],
        compiler_params=pltpu.CompilerParams(dimension_semantics=("arbitrary",), vmem_limit_bytes=VMEM_LIMIT),
        name="proj",
    )(x2, cos_t, sin_t, tri, *consts)


def _attn_kernel(q_ref, km_ref, ke_ref, v_ref, o_ref, *, hb, t, s_len, combos):
    rows = hb * t
    row_in_head = lax.broadcasted_iota(jnp.int32, (rows, t), 0) % t
    causal = lax.broadcasted_iota(jnp.int32, (rows, t), 1) <= row_in_head
    lo = lax.broadcasted_iota(jnp.int32, (1, LANE), 1) < 64

    def block(qs, c0, carry, masked):
        m, l, acc = carry
        kc = jnp.concatenate([km_ref[pl.ds(c0, t), :], ke_ref[pl.ds(c0, t), :]], axis=1)
        s = _dot_nt(qs, kc)
        if masked:
            s = jnp.where(causal, s, NEG)
        m_new = jnp.maximum(m, jnp.max(s, axis=1, keepdims=True))
        alpha = jnp.exp(m - m_new)
        p = jnp.exp(s - m_new)
        l = alpha * l + jnp.sum(p, axis=1, keepdims=True)
        acc = alpha * acc + _dot(p.astype(BF16), v_ref[pl.ds(c0, t), :])
        return m_new, l, acc

    def q_body(qb, _):
        r0 = pl.multiple_of(qb * t, t)
        qs = jnp.concatenate([q_ref[pl.ds(r0, t), h * SLAB:(h + 1) * SLAB] for h in range(hb)], axis=0)
        init = (jnp.full((rows, 1), NEG, F32), jnp.zeros((rows, 1), F32), jnp.zeros((rows, LANE), F32))
        carry = lax.fori_loop(0, qb, lambda kb, c: block(qs, pl.multiple_of(kb * t, t), c, False), init)
        _, l, acc = block(qs, r0, carry, True)
        o = acc * (1.0 / l)
        out = [jnp.where(lo, o[a * t:(a + 1) * t], o[b * t:(b + 1) * t]) for a, b in combos]
        o_ref[pl.ds(r0, t), :] = jnp.concatenate(out, axis=1).astype(o_ref.dtype)
        return 0

    lax.fori_loop(0, s_len // t, q_body, 0)


def _attention(q, km, ke, v, batch, s_len, hb, combos, name):
    n_pairs = km.shape[1] // LANE
    ow = len(combos) * LANE
    return pl.pallas_call(
        functools.partial(_attn_kernel, hb=hb, t=ATTN_T, s_len=s_len, combos=combos),
        grid=(batch, n_pairs),
        in_specs=[pl.BlockSpec((s_len, hb * SLAB), lambda b, p: (b, p)),
                  pl.BlockSpec((s_len, LANE), lambda b, p: (b, p)),
                  pl.BlockSpec((s_len, LANE), lambda b, p: (b, 0)),
                  pl.BlockSpec((s_len, LANE), lambda b, p: (b, p))],
        out_specs=pl.BlockSpec((s_len, ow), lambda b, p: (b, p)),
        out_shape=jax.ShapeDtypeStruct((batch * s_len, n_pairs * ow), BF16),
        compiler_params=pltpu.CompilerParams(dimension_semantics=("arbitrary", "arbitrary"),
                                             vmem_limit_bytes=VMEM_LIMIT),
        name=name,
    )(q, km, ke, v)


def _memkv_kernel(mem_ref, gmem_ref, w_ref, gmk_ref, kt_ref, vt_ref):
    mn = (_rms(mem_ref[...], D_MODEL) * gmem_ref[...]).astype(BF16)
    kv = _dot(mn, w_ref[...])
    k = _group64_rms(kv[:, 0:MEM_W]) * gmk_ref[...]
    kt_ref[0] = k.T
    vt_ref[0] = kv[:, MEM_W:].T


def _memkv(mem2, batch, mem_len, w):
    return pl.pallas_call(
        _memkv_kernel,
        grid=(batch,),
        in_specs=[pl.BlockSpec((mem_len, D_MODEL), lambda b: (b, 0)), _const_spec(w['g_mem'].shape),
                  _const_spec(w['w_mkv'].shape), _const_spec(w['g_mk'].shape)],
        out_specs=[pl.BlockSpec((1, MEM_W, mem_len), lambda b: (b, 0, 0))] * 2,
        out_shape=[jax.ShapeDtypeStruct((batch, MEM_W, mem_len), F32)] * 2,
        compiler_params=pltpu.CompilerParams(dimension_semantics=("arbitrary",), vmem_limit_bytes=VMEM_LIMIT),
        name="memkv",
    )(mem2, w['g_mem'], w['w_mkv'], w['g_mk'])


def _finish_kernel(x_ref, om_ref, mk_ref, mv_ref, wo_ref, gcross_ref, wmq_ref, gmq_ref, wmo_ref, gffn_ref,
                   wup_ref, wdown_ref, y_ref, q_scr, o_scr, *, spt, ts):
    h = x_ref[...] + _dot(om_ref[...].astype(BF16), wo_ref[...])
    hn = (_rms(h, D_MODEL) * gcross_ref[...]).astype(BF16)
    q_scr[...] = _group64_rms(_dot(hn, wmq_ref[...])) * gmq_ref[...]
    head_of_lane = lax.broadcasted_iota(jnp.int32, (1, MEM_W), 1) // MEM_DIM

    def one_seq(s, _):
        r0 = pl.multiple_of(s * ts, ts)
        q = q_scr[pl.ds(r0, ts), :]
        qs = jnp.concatenate([jnp.where(head_of_lane == hd, q, 0.0) for hd in range(MEM_HEADS)], axis=0)
        sc = _dot(qs.astype(BF16), mk_ref[s].astype(BF16))
        p = jnp.exp(sc - jnp.max(sc, axis=1, keepdims=True))
        l = jnp.sum(p, axis=1, keepdims=True)
        pv = _dot_nt(p.astype(BF16), mv_ref[s].astype(BF16)) * (1.0 / l)
        o = jnp.where(head_of_lane == 0, pv[0:ts], 0.0)
        for hd in range(1, MEM_HEADS):
            o = o + jnp.where(head_of_lane == hd, pv[hd * ts:(hd + 1) * ts], 0.0)
        o_scr[pl.ds(r0, ts), :] = o
        return 0

    if spt == 1:
        one_seq(0, 0)
    else:
        lax.fori_loop(0, spt, one_seq, 0)

    h = h + _dot(o_scr[...].astype(BF16), wmo_ref[...])
    hn = (_rms(h, D_MODEL) * gffn_ref[...]).astype(BF16)
    y = h
    for c in range(D_FF // FFN_CHUNK):
        u = jnp.maximum(_dot(hn, wup_ref[:, c * FFN_CHUNK:(c + 1) * FFN_CHUNK]), 0.0)
        y = y + _dot((u * u).astype(BF16), wdown_ref[c * FFN_CHUNK:(c + 1) * FFN_CHUNK, :])
    y_ref[...] = y


def _finish(x2, om2, mk, mv, w_o, w, spt, ts, tiles_per_seq, name):
    assert spt == 1 or tiles_per_seq == 1
    n = x2.shape[0]
    tm = spt * ts
    mem_len = mk.shape[2]
    mem_spec = pl.BlockSpec((spt, MEM_W, mem_len), lambda i: (i // tiles_per_seq, 0, 0))
    consts = [w_o, w['g_cross'], w['w_mq'], w['g_mq'], w['w_mo'], w['g_ffn'], w['w_up'], w['w_down']]
    return pl.pallas_call(
        functools.partial(_finish_kernel, spt=spt, ts=ts),
        grid=(n // tm,),
        in_specs=[pl.BlockSpec((tm, D_MODEL), lambda i: (i, 0)), pl.BlockSpec((tm, D_MODEL), lambda i: (i, 0)),
                  mem_spec, mem_spec] + [_const_spec(c.shape) for c in consts],
        out_specs=pl.BlockSpec((tm, D_MODEL), lambda i: (i, 0)),
        out_shape=jax.ShapeDtypeStruct((n, D_MODEL), F32),
        scratch_shapes=[pltpu.VMEM((tm, MEM_W), F32), pltpu.VMEM((tm, MEM_W), F32)],
        compiler_params=pltpu.CompilerParams(dimension_semantics=("arbitrary",), vmem_limit_bytes=VMEM_LIMIT),
        name=name,
    )(x2, om2, mk, mv, *consts)


def _decode_kernel(pt_ref, qm_ref, fq_ref, cum_ref, locm_ref, lock_ref, locv_ref, locf_ref,
                   wkt_ref, wv_ref, gkn_ref, u_ref, linc_ref, *rest, nj):
    mla = rest[0:PPS]
    fk = rest[PPS:2 * PPS]
    fv = rest[2 * PPS:3 * PPS]
    lf = rest[3 * PPS:4 * PPS]
    o_ref = rest[4 * PPS]
    lhs_s, qr_s, qbd_s, carry_s, m1, l1, ctx, m2, l2, acc2 = rest[4 * PPS + 1:]
    j = pl.program_id(1)
    nrow = MLA_HEADS * DEC_T
    cum8 = cum_ref[0]

    def mla_scores(lat_t, rope_t):
        big = _dot(lhs_s[...], lat_t)
        rs = _dot(qr_s[...], rope_t)
        slabs = []
        for h in range(MLA_HEADS):
            kvh = big[h * NOPE:(h + 1) * NOPE]
            r = lax.rsqrt(jnp.sum(kvh * kvh, axis=0, keepdims=True) * (1.0 / NOPE) + EPS)
            base = MLA_HEADS * NOPE + h * DEC_T
            slabs.append(big[base:base + DEC_T] * r + rs[h * DEC_T:(h + 1) * DEC_T])
        return jnp.concatenate(slabs, axis=0)

    def fox_scores(k_t, bias8):
        s = _dot(qbd_s[...], k_t)
        slabs = []
        for h in range(FOX_HEADS):
            slabs.append(s[h * DEC_T:(h + 1) * DEC_T] + bias8[h:h + 1, :] + cum8[:, h:h + 1])
        return jnp.concatenate(slabs, axis=0)

    def update(s, m_ref, l_ref, acc_ref, vals_t):
        m_old = m_ref[...]
        m_new = jnp.maximum(m_old, jnp.max(s, axis=1, keepdims=True))
        alpha = jnp.exp(m_old - m_new)
        p = jnp.exp(s - m_new)
        l_ref[...] = alpha * l_ref[...] + jnp.sum(p, axis=1, keepdims=True)
        m_ref[...] = m_new
        acc_ref[...] = alpha * acc_ref[...] + _dot_nt(p.astype(BF16), vals_t)

    @pl.when(j == 0)
    def _():
        qm8 = qm_ref[0].astype(F32)
        qn = jnp.concatenate([qm8[:, (2 * p) * SLAB:(2 * p) * SLAB + LANE]
                              + qm8[:, (2 * p + 1) * SLAB:(2 * p + 1) * SLAB + LANE] for p in range(4)], axis=1)
        qn = qn * gkn_ref[...]
        head512 = lax.broadcasted_iota(jnp.int32, (1, MLA_HEADS * NOPE), 1) // NOPE
        qn_bd = jnp.concatenate([jnp.where(head512 == h, qn, 0.0) for h in range(MLA_HEADS)], axis=0)
        lhs_s[0:MLA_HEADS * NOPE, :] = wkt_ref[...]
        lhs_s[MLA_HEADS * NOPE:, :] = _dot(qn_bd.astype(BF16), wkt_ref[...]).astype(BF16)
        qr_s[...] = jnp.concatenate([qm8[:, h * SLAB + LANE:h * SLAB + LANE + ROPE]
                                     for h in range(MLA_HEADS)], axis=0).astype(BF16)
        fq8 = fq_ref[0].astype(F32)
        zero = jnp.zeros((DEC_T, LANE), F32)
        blocks = []
        for h in range(FOX_HEADS):
            slab = fq8[:, h * SLAB:h * SLAB + LANE]
            blocks.append(jnp.concatenate([slab, zero] if (h // 2) // 2 == 0 else [zero, slab], axis=1))
        qbd_s[...] = jnp.concatenate(blocks, axis=0).astype(BF16)
        carry_s[...] = jnp.zeros_like(carry_s)
        m1[...] = jnp.full(m1.shape, NEG, F32)
        m2[...] = jnp.full(m2.shape, NEG, F32)
        l1[...] = jnp.zeros_like(l1)
        l2[...] = jnp.zeros_like(l2)
        ctx[...] = jnp.zeros_like(ctx)
        acc2[...] = jnp.zeros_like(acc2)

        tok = lax.broadcasted_iota(jnp.int32, (nrow, LANE), 0) % DEC_T
        valid = lax.broadcasted_iota(jnp.int32, (nrow, LANE), 1) <= tok
        lat_t = locm_ref[0, 0:KV_LORA, :].astype(BF16)
        s1 = mla_scores(lat_t, locm_ref[0, KV_LORA:MLA_ROW, :].astype(BF16))
        update(jnp.where(valid, s1, NEG), m1, l1, ctx, lat_t)
        cum_t = _dot3(locf_ref[0], linc_ref[...])
        s2 = fox_scores(lock_ref[0].astype(BF16), -cum_t)
        update(jnp.where(valid, s2, NEG), m2, l2, acc2, locv_ref[0].astype(BF16))

    lat_t = jnp.concatenate([mla[i][0, 0:KV_LORA, :].astype(BF16) for i in range(PPS)], axis=1)
    rope_t = jnp.concatenate([mla[i][0, KV_LORA:MLA_ROW, :].astype(BF16) for i in range(PPS)], axis=1)
    update(mla_scores(lat_t, rope_t), m1, l1, ctx, lat_t)

    lfs = jnp.concatenate([lf[i][0] for i in range(PPS)], axis=0)
    res = _dot3(lfs, u_ref[...])
    carry = carry_s[...]
    biases = [None] * PPS
    for i in reversed(range(PPS)):
        biases[i] = res[i * 8:(i + 1) * 8, 0:LANE] + carry
        carry = carry + res[i * 8:(i + 1) * 8, LANE:2 * LANE]
    carry_s[...] = carry
    k_t = jnp.concatenate([fk[i][0].astype(BF16) for i in range(PPS)], axis=1)
    v_t = jnp.concatenate([fv[i][0].astype(BF16) for i in range(PPS)], axis=1)
    update(fox_scores(k_t, jnp.concatenate(biases, axis=1)), m2, l2, acc2, v_t)

    @pl.when(j == nj - 1)
    def _():
        full = _dot((ctx[...] * (1.0 / l1[...])).astype(BF16), wv_ref[...])
        head512 = lax.broadcasted_iota(jnp.int32, (1, MLA_HEADS * MLA_V), 1) // MLA_V
        o_mla = jnp.where(head512 == 0, full[0:DEC_T], 0.0)
        for h in range(1, MLA_HEADS):
            o_mla = o_mla + jnp.where(head512 == h, full[h * DEC_T:(h + 1) * DEC_T], 0.0)
        a2 = acc2[...] * (1.0 / l2[...])
        head256 = lax.broadcasted_iota(jnp.int32, (1, FOX_KV * FOX_DIM), 1) // FOX_DIM
        outs = []
        for grp in range(FOX_HEADS // FOX_KV):
            o = None
            for kvh in range(FOX_KV):
                r0 = (kvh * 2 + grp) * DEC_T
                term = jnp.where(head256 == kvh, a2[r0:r0 + DEC_T], 0.0)
                o = term if o is None else o + term
            outs.append(o)
        o_ref[0] = jnp.concatenate([o_mla] + outs, axis=1)


def _decode(page_table_flat, qm, fq, cum, locm, lock, locv, locf, cm, ck, cv, cl, w, n_seq, n_pages):
    nj = n_pages // PPS

    def page_spec(rows, i):
        return pl.BlockSpec((1, rows, LANE), lambda b, j, pt: (pt[b * n_pages + (nj - 1 - j) * PPS + i], 0, 0))

    seq = lambda rows, width: pl.BlockSpec((1, rows, width), lambda b, j, pt: (b, 0, 0))
    cspec = lambda a: pl.BlockSpec(a.shape, lambda b, j, pt: (0,) * a.ndim)
    consts = [w['w_kt'], w['w_v'], w['g_kn512'], w['u_mat'], w['l_inc']]
    nrow = MLA_HEADS * DEC_T
    grid_spec = pltpu.PrefetchScalarGridSpec(
        num_scalar_prefetch=1,
        grid=(n_seq, nj),
        in_specs=[seq(DEC_T, MLA_HEADS * SLAB), seq(DEC_T, FOX_HEADS * SLAB), seq(DEC_T, LANE),
                  seq(MLA_ROW, LANE), seq(FOX_KV * FOX_DIM, LANE), seq(FOX_KV * FOX_DIM, LANE), seq(FOX_HEADS, LANE)]
        + [cspec(c) for c in consts]
        + [page_spec(MLA_ROW, i) for i in range(PPS)] + [page_spec(FOX_KV * FOX_DIM, i) for i in range(PPS)]
        + [page_spec(FOX_KV * FOX_DIM, i) for i in range(PPS)] + [page_spec(FOX_HEADS, i) for i in range(PPS)],
        out_specs=pl.BlockSpec((1, DEC_T, D_MODEL), lambda b, j, pt: (b, 0, 0)),
        scratch_shapes=[pltpu.VMEM((MLA_HEADS * NOPE + nrow, KV_LORA), BF16), pltpu.VMEM((nrow, ROPE), BF16),
                        pltpu.VMEM((nrow, FOX_KV * FOX_DIM), BF16), pltpu.VMEM((FOX_HEADS, LANE), F32),
                        pltpu.VMEM((nrow, 1), F32), pltpu.VMEM((nrow, 1), F32), pltpu.VMEM((nrow, KV_LORA), F32),
                        pltpu.VMEM((nrow, 1), F32), pltpu.VMEM((nrow, 1), F32),
                        pltpu.VMEM((nrow, FOX_KV * FOX_DIM), F32)])
    return pl.pallas_call(
        functools.partial(_decode_kernel, nj=nj),
        grid_spec=grid_spec,
        out_shape=jax.ShapeDtypeStruct((n_seq, DEC_T, D_MODEL), F32),
        compiler_params=pltpu.CompilerParams(dimension_semantics=("arbitrary", "arbitrary"),
                                             vmem_limit_bytes=VMEM_LIMIT),
        name="decode",
    )(page_table_flat, qm, fq, cum, locm, lock, locv, locf, *consts, *([cm] * PPS), *([ck] * PPS), *([cv] * PPS),
      *([cl] * PPS))


def _place(vec, offset, width=LANE):
    return jnp.zeros((width,), F32).at[offset:offset + vec.shape[0]].set(vec)


def _rot_half(v, axis=-1):
    a, b = jnp.split(v, 2, axis=axis)
    return jnp.concatenate([-b, a], axis=axis)


def _prep_weights(g_mix, w_in, g_q_lat, w_q_up, g_kv_lat, g_k_rope, g_q_nope, g_q_rope, g_k_nope, w_kv_up, g_fox_q,
                  g_fox_k, b_forget, w_o, g_cross, g_mem, w_mq, w_mk, w_mv, g_mq, g_mk, w_mo, g_ffn, w_up, w_down):
    w = {}
    win = w_in[0]
    seg = [win[:, IN_SPLITS[k]:IN_SPLITS[k + 1]] for k in range(7)]
    c_q, c_kv, k_r, f_q, f_k, f_v, f_l = seg
    zc = lambda n: jnp.zeros((D_MODEL, n), F32)
    cols = [c_q, c_kv, k_r, zc(LANE - ROPE), _rot_half(k_r), zc(LANE - ROPE)]
    for h in range(FOX_HEADS):
        off = ((h // 2) % 2) * FOX_DIM
        cols += [zc(off), f_q[:, h * FOX_DIM:(h + 1) * FOX_DIM], zc(LANE - FOX_DIM - off)]
    cols += [f_k, f_v, f_l, zc(LANE - FOX_HEADS)]
    w['w_in'] = jnp.concatenate([c for c in cols if c.shape[1]], axis=1).astype(BF16)

    wq = w_q_up[0]
    zq = lambda n: jnp.zeros((Q_LORA, n), F32)
    qcols, rcols = [], []
    for h in range(MLA_HEADS):
        base = h * (NOPE + ROPE)
        off = (h % 2) * NOPE
        rope_w = wq[:, base + NOPE:base + NOPE + ROPE]
        qcols += [zq(off), wq[:, base:base + NOPE], zq(LANE - NOPE - off), rope_w, zq(LANE - ROPE)]
        rcols += [_rot_half(rope_w), zq(LANE - ROPE)]
    w['w_q'] = jnp.concatenate([c for c in qcols + rcols if c.shape[1]], axis=1).astype(BF16)

    wkv = w_kv_up[0].reshape(KV_LORA, MLA_HEADS, NOPE + MLA_V)
    w_k = wkv[:, :, :NOPE].reshape(KV_LORA, MLA_HEADS * NOPE)
    w_v = wkv[:, :, NOPE:].reshape(KV_LORA, MLA_HEADS * MLA_V)
    w['w_kv'] = jnp.concatenate([w_k, w_v], axis=1).astype(BF16)
    w['w_kt'] = w_k.T.astype(BF16)
    w['w_v'] = w_v.astype(BF16)
    w['g_kn512'] = jnp.tile(g_k_nope[0], MLA_HEADS)[None, :]

    gqr, gkr = g_q_rope[0] * MLA_SCALE, g_k_rope[0]
    swap = lambda v: jnp.concatenate([v[HALF:], v[:HALF]])
    k_aug_base = jnp.zeros((LANE,), F32).at[0:3].set(1.0)
    rows = [_place(g_q_nope[0] * MLA_SCALE, 0), _place(g_q_nope[0] * MLA_SCALE, NOPE), _place(gqr, 0),
            _place(swap(gqr), 0), _place(gkr, 0), _place(swap(gkr), 0), jnp.tile(g_k_nope[0], 2),
            _place(g_fox_q[0] * FOX_SCALE, 0), _place(g_fox_q[0] * FOX_SCALE, FOX_DIM), jnp.tile(g_fox_k[0], 2),
            _place(b_forget[0], 0), k_aug_base]
    rows += [jnp.zeros((LANE,), F32)] * (16 - len(rows))
    w['g128'] = jnp.stack(rows)

    place = np.zeros((3, LANE, LANE + FOX_HEADS * LANE), np.float32)
    base_q = np.zeros((1, FOX_HEADS * LANE), np.float32)
    for piece in range(3):
        for h in range(FOX_HEADS):
            place[piece, h, 3 + 3 * h + piece] = -1.0
            place[piece, h, LANE + h * LANE + piece] = 1.0
            base_q[0, h * LANE + 3 + 3 * h + piece] = 1.0
    w['place'] = jnp.asarray(place, BF16)
    w['base_q'] = jnp.asarray(base_q)

    w['g_mix'], w['g_q_lat'], w['g_kv_lat'] = g_mix, g_q_lat, g_kv_lat
    w['g_cross'], w['g_mem'], w['g_ffn'] = g_cross, g_mem, g_ffn
    w['g_mq'] = jnp.tile(g_mq[0] * MEM_SCALE, MEM_HEADS)[None, :]
    w['g_mk'] = jnp.tile(g_mk[0], MEM_HEADS)[None, :]
    w['w_mkv'] = jnp.concatenate([w_mk[0], w_mv[0]], axis=1).astype(BF16)
    w['w_mq'] = w_mq[0].astype(BF16)
    w['w_mo'] = w_mo[0].astype(BF16)
    w['w_up'] = w_up[0].astype(BF16)
    w['w_down'] = w_down[0].astype(BF16)

    wo = w_o[0]
    mla_w = MLA_HEADS * MLA_V
    fox_rows = wo[mla_w:].reshape(FOX_KV // 2, 2, 2, FOX_DIM, D_MODEL)
    prompt_fox = jnp.transpose(fox_rows, (0, 2, 1, 3, 4)).reshape(FOX_HEADS * FOX_DIM, D_MODEL)
    w['w_o'] = jnp.concatenate([wo[:mla_w], prompt_fox], axis=0).astype(BF16)
    fox_rows = wo[mla_w:].reshape(FOX_KV, 2, FOX_DIM, D_MODEL)
    sample_fox = jnp.transpose(fox_rows, (1, 0, 2, 3)).reshape(FOX_HEADS * FOX_DIM, D_MODEL)
    w['w_o_sample'] = jnp.concatenate([wo[:mla_w], sample_fox], axis=0).astype(BF16)

    idx = np.arange(LANE)
    u_strict = (idx[:, None] > idx[None, :]).astype(np.float32)
    w['u_mat'] = jnp.asarray(np.concatenate([u_strict, np.ones((LANE, LANE), np.float32)], axis=1), BF16)
    w['l_inc'] = jnp.asarray((idx[:, None] <= idx[None, :]).astype(np.float32), BF16)
    return w


def _rope_tables(pos):
    inv_freq = ROPE_BASE ** (-jnp.arange(HALF, dtype=F32) / HALF)
    ang = pos.astype(F32)[:, None] * inv_freq[None, :]
    pad = jnp.zeros((pos.shape[0], LANE - ROPE), F32)
    cos = jnp.concatenate([jnp.cos(ang), jnp.cos(ang), pad], axis=1)
    sin = jnp.concatenate([jnp.sin(ang), jnp.sin(ang), pad], axis=1)
    return cos, sin


def _tri(tm, group):
    idx = np.arange(tm)
    m = (idx[None, :] <= idx[:, None]) & (idx[None, :] // group == idx[:, None] // group)
    return jnp.asarray(m.astype(np.float32), BF16)


def kernel(x_prompt, x_sample, cache_mla, cache_fox_k, cache_fox_v, cache_fox_logf, cache_mem_k, cache_mem_v,
           page_table, mem_prompt, g_mix, w_in, g_q_lat, w_q_up, g_kv_lat, g_k_rope, g_q_nope, g_q_rope, g_k_nope,
           w_kv_up, g_fox_q, g_fox_k, b_forget, w_o, g_cross, g_mem, w_mq, w_mk, w_mv, g_mq, g_mk, w_mo, g_ffn, w_up,
           w_down):
    assert cache_mla.shape[0] == 1, "single-layer kernel"
    batch, s_len, _ = x_prompt.shape
    n_seq, t_new, _ = x_sample.shape
    n_pool, page = cache_mla.shape[1], cache_mla.shape[2]
    n_pages = page_table.shape[1]
    mem_len = mem_prompt.shape[1]
    assert page == LANE and t_new <= DEC_T and n_pages % PPS == 0
    assert s_len % PROJ_TM == 0 and s_len % ATTN_T == 0 and s_len % FIN_TM == 0
    w = _prep_weights(g_mix, w_in, g_q_lat, w_q_up, g_kv_lat, g_k_rope, g_q_nope, g_q_rope, g_k_nope, w_kv_up,
                      g_fox_q, g_fox_k, b_forget, w_o, g_cross, g_mem, w_mq, w_mk, w_mv, g_mq, g_mk, w_mo, g_ffn,
                      w_up, w_down)

    n = batch * s_len
    xp = x_prompt.reshape(n, D_MODEL)
    cos_p, sin_p = _rope_tables(jnp.arange(s_len))
    (p_row, qm, kn, vm, kr, fq, fk, fkb, fv, fvb, p_lf, _, kaug) = _proj(
        xp, cos_p, sin_p, _tri(PROJ_TM, PROJ_TM), w, s_len // PROJ_TM)
    o_mla = _attention(qm, kn, kr, vm, batch, s_len, 2, ((0, 1),), "attn_mla")
    o_fox = _attention(fq, fkb, kaug, fvb, batch, s_len, 4, ((0, 2), (1, 3)), "attn_fox")
    mk_t, mv_t = _memkv(mem_prompt.reshape(batch * mem_len, D_MODEL), batch, mem_len, w)
    y_prompt = _finish(xp, jnp.concatenate([o_mla, o_fox], axis=1), mk_t, mv_t, w['w_o'], w, 1, FIN_TM,
                       s_len // FIN_TM, "finish_prompt")

    ns = n_seq * DEC_T
    xs = jnp.pad(x_sample, ((0, 0), (0, DEC_T - t_new), (0, 0))).reshape(ns, D_MODEL)
    pos_s = n_pages * page + (jnp.arange(PROJ_TM) % DEC_T)
    cos_s, sin_s = _rope_tables(pos_s)
    (s_row, qm_s, _, _, _, fq_s, fk_s, _, fv_s, _, s_lf, cum_s, _) = _proj(
        xs, cos_s, sin_s, _tri(PROJ_TM, DEC_T), w, 1)

    def new_page(a):
        a = a.reshape(n_seq, DEC_T, -1)[:, :t_new]
        return jnp.pad(jnp.transpose(a, (0, 2, 1)), ((0, 0), (0, 0), (0, LANE - t_new)))

    cm = jnp.transpose(cache_mla[0], (0, 2, 1))
    ck = jnp.transpose(cache_fox_k[0], (0, 2, 3, 1)).reshape(n_pool, FOX_KV * FOX_DIM, page)
    cv = jnp.transpose(cache_fox_v[0], (0, 2, 3, 1)).reshape(n_pool, FOX_KV * FOX_DIM, page)
    cl = jnp.transpose(cache_fox_logf[0], (0, 2, 1))
    o_s = _decode(page_table.reshape(-1), qm_s.reshape(n_seq, DEC_T, -1), fq_s.reshape(n_seq, DEC_T, -1),
                  cum_s.reshape(n_seq, DEC_T, LANE), new_page(s_row), new_page(fk_s), new_page(fv_s), new_page(s_lf),
                  cm, ck, cv, cl, w, n_seq, n_pages)
    cmk = jnp.transpose(cache_mem_k[0], (0, 2, 3, 1)).reshape(n_seq, MEM_W, mem_len)
    cmv = jnp.transpose(cache_mem_v[0], (0, 2, 3, 1)).reshape(n_seq, MEM_W, mem_len)
    y_s = _finish(xs, o_s.reshape(ns, D_MODEL), cmk, cmv, w['w_o_sample'], w, 16, DEC_T, 1, "finish_sample")

    def tok(a, *shape):
        return a.reshape(n_seq, DEC_T, -1)[:, :t_new].reshape(1, n_seq, t_new, *shape)

    mem_out = lambda a: jnp.transpose(a.reshape(1, batch, MEM_HEADS, MEM_DIM, mem_len), (0, 1, 4, 2, 3))
    return (y_prompt.reshape(batch, s_len, D_MODEL), tok(y_s, D_MODEL)[0],
            p_row.reshape(1, batch, s_len, MLA_ROW), fk.reshape(1, batch, s_len, FOX_KV, FOX_DIM),
            fv.reshape(1, batch, s_len, FOX_KV, FOX_DIM), p_lf.reshape(1, batch, s_len, FOX_HEADS),
            mem_out(mk_t), mem_out(mv_t),
            tok(s_row, MLA_ROW), tok(fk_s, FOX_KV, FOX_DIM), tok(fv_s, FOX_KV, FOX_DIM), tok(s_lf, FOX_HEADS))
```

```python
import functools

import numpy as np
import jax
import jax.numpy as jnp
from jax import lax
from jax.experimental import pallas as pl
from jax.experimental.pallas import tpu as pltpu

F32 = jnp.float32
BF16 = jnp.bfloat16

D_MODEL = 1024
Q_LORA = 384
KV_LORA = 256
ROPE = 32
HALF = ROPE // 2
NOPE = 64
MLA_V = 64
MLA_HEADS = 8
MLA_ROW = KV_LORA + ROPE
FOX_HEADS = 8
FOX_KV = 4
FOX_DIM = 64
MEM_HEADS = 4
MEM_DIM = 64
MEM_W = MEM_HEADS * MEM_DIM
D_FF = 4096
IN_SPLITS = (0, 384, 640, 672, 1184, 1440, 1696, 1704)
ROPE_BASE = 10000.0
EPS = 1e-6
NEG = -1e30
MLA_SCALE = (NOPE + ROPE) ** -0.5
FOX_SCALE = FOX_DIM ** -0.5
MEM_SCALE = MEM_DIM ** -0.5

LANE = 128
SLAB = 2 * LANE
VMEM_LIMIT = 56 * 1024 * 1024
PROJ_TM = 256
ATTN_T = 512
FIN_TM = 512
DEC_T = 8
PPS = 8
FFN_CHUNK = 1024
ONES_ROWS = 16

C_CQ, C_CKV, C_KR, C_KROT, C_FQ, C_FK, C_FV, C_FL, W_IN_EXP = 0, 384, 640, 768, 896, 1920, 2176, 2432, 2560
G_QN_E, G_QN_O, G_QR, G_QR_ROT, G_KR, G_KR_ROT, G_KN, G_FQ_E, G_FQ_O, G_FK, G_BF, G_KAUG = range(12)


def _const_spec(shape):
    n = len(shape)
    return pl.BlockSpec(shape, lambda *_: (0,) * n, pipeline_mode=pl.Buffered(1))


def _rms(x, width):
    return x * lax.rsqrt(jnp.sum(x * x, axis=-1, keepdims=True) * (1.0 / width) + EPS)


def _group64_rms(x):
    out = []
    lane = lax.broadcasted_iota(jnp.int32, (1, LANE), 1)
    lo = lane < 64
    for j in range(x.shape[1] // LANE):
        s = x[:, j * LANE:(j + 1) * LANE]
        s2 = s * s
        r_lo = lax.rsqrt(jnp.sum(jnp.where(lo, s2, 0.0), axis=-1, keepdims=True) * (1.0 / 64) + EPS)
        r_hi = lax.rsqrt(jnp.sum(jnp.where(lo, 0.0, s2), axis=-1, keepdims=True) * (1.0 / 64) + EPS)
        out.append(s * jnp.where(lo, r_lo, r_hi))
    return out[0] if len(out) == 1 else jnp.concatenate(out, axis=1)


def _split3(x):
    hi = x.astype(BF16)
    r1 = x - hi.astype(F32)
    mid = r1.astype(BF16)
    lo = (r1 - mid.astype(F32)).astype(BF16)
    return hi, mid, lo


def _dot(a, b):
    return jnp.dot(a, b, preferred_element_type=F32)


def _dot_nt(a, b):
    return lax.dot_general(a, b, (((1,), (1,)), ((), ())), preferred_element_type=F32)


def _dot3(x, w):
    hi, mid, lo = _split3(x)
    return _dot(hi, w) + _dot(mid, w) + _dot(lo, w)


def _proj_kernel(x_ref, cos_ref, sin_ref, tri_ref, gmix_ref, win_ref, gql_ref, wq_ref, gkv_ref, wkv_ref,
                 g128_ref, place_ref, baseq_ref,
                 rowt_ref, qm_ref, kn_ref, vmt_ref, kr_ref, fq_ref, fkt_ref, fkb_ref, fvt_ref, fvbt_ref,
                 lft_ref, cum_ref, kaug_ref, carry_ref, *, tiles_per_seq):
    i = pl.program_id(0)
    cos = cos_ref[...]
    sin = sin_ref[...]

    def g(r):
        return g128_ref[r:r + 1, :]

    xn = (_rms(x_ref[...], D_MODEL) * gmix_ref[...]).astype(BF16)
    a = _dot(xn, win_ref[...])

    cq = (_rms(a[:, C_CQ:C_CQ + Q_LORA], Q_LORA) * gql_ref[...]).astype(BF16)
    q = _dot(cq, wq_ref[...])
    for h in range(MLA_HEADS):
        nope = q[:, h * SLAB:h * SLAB + LANE]
        rope = q[:, h * SLAB + LANE:(h + 1) * SLAB]
        rot = q[:, MLA_HEADS * SLAB + h * LANE:MLA_HEADS * SLAB + (h + 1) * LANE]
        rn = lax.rsqrt(jnp.sum(nope * nope, axis=-1, keepdims=True) * (1.0 / NOPE) + EPS)
        rr = lax.rsqrt(jnp.sum(rope * rope, axis=-1, keepdims=True) * (1.0 / ROPE) + EPS)
        qm_ref[:, h * SLAB:h * SLAB + LANE] = (nope * rn * g(G_QN_O if h % 2 else G_QN_E)).astype(BF16)
        qm_ref[:, h * SLAB + LANE:(h + 1) * SLAB] = (
            rr * (rope * g(G_QR) * cos + rot * g(G_QR_ROT) * sin)).astype(BF16)

    lat = _rms(a[:, C_CKV:C_CKV + KV_LORA], KV_LORA) * gkv_ref[...]
    kr = a[:, C_KR:C_KR + LANE]
    krot = a[:, C_KROT:C_KROT + LANE]
    rk = lax.rsqrt(jnp.sum(kr * kr, axis=-1, keepdims=True) * (1.0 / ROPE) + EPS)
    k_rope = rk * (kr * g(G_KR) * cos + krot * g(G_KR_ROT) * sin)
    rowt_ref[0:KV_LORA, :] = lat.T
    rowt_ref[KV_LORA:MLA_ROW, :] = k_rope.T[0:ROPE]
    kr_ref[...] = k_rope.astype(BF16)

    kv = _dot(lat.astype(BF16), wkv_ref[...])
    gkn = g(G_KN)
    kn = _group64_rms(kv[:, 0:MLA_HEADS * NOPE])
    kn_ref[...] = (kn * jnp.concatenate([gkn] * 4, axis=1)).astype(BF16)
    vmt_ref[...] = kv[:, MLA_HEADS * NOPE:].T.astype(BF16)

    for h in range(FOX_HEADS):
        s = a[:, C_FQ + h * LANE:C_FQ + (h + 1) * LANE]
        r = lax.rsqrt(jnp.sum(s * s, axis=-1, keepdims=True) * (1.0 / FOX_DIM) + EPS)
        fq_ref[:, h * SLAB:h * SLAB + LANE] = (s * r * g(G_FQ_O if (h // 2) % 2 else G_FQ_E)).astype(BF16)
    fk = _group64_rms(a[:, C_FK:C_FK + 2 * LANE]) * jnp.concatenate([g(G_FK)] * 2, axis=1)
    fkt_ref[...] = fk.T
    fkb_ref[...] = fk.astype(BF16)
    fvt = a[:, C_FV:C_FV + 2 * LANE].T
    fvt_ref[...] = fvt
    fvbt_ref[...] = fvt.astype(BF16)

    z = a[:, C_FL:C_FL + LANE] + g(G_BF)
    lane = lax.broadcasted_iota(jnp.int32, (1, LANE), 1)
    logf = jnp.where(lane < FOX_HEADS, jnp.minimum(z, 0.0) - jnp.log1p(jnp.exp(-jnp.abs(z))), 0.0)
    lft_ref[...] = logf.T[0:FOX_HEADS]

    @pl.when(i % tiles_per_seq == 0)
    def _():
        carry_ref[...] = jnp.zeros_like(carry_ref)

    l_hi, l_mid, l_lo = _split3(logf)
    tri = tri_ref[...]
    cum = _dot(tri, l_hi) + _dot(tri, l_mid) + _dot(tri, l_lo) + carry_ref[0:1, :]
    cum_ref[...] = cum
    tm = cum.shape[0]
    carry_ref[...] = jnp.broadcast_to(cum[tm - 1:tm, :], carry_ref.shape)

    c_hi, c_mid, c_lo = _split3(cum)
    aug = _dot(c_hi, place_ref[0]) + _dot(c_mid, place_ref[1]) + _dot(c_lo, place_ref[2])
    kaug_ref[...] = (aug[:, 0:LANE] + g(G_KAUG)).astype(BF16)
    for h in range(FOX_HEADS):
        fq_ref[:, h * SLAB + LANE:(h + 1) * SLAB] = (
            aug[:, (h + 1) * LANE:(h + 2) * LANE] + baseq_ref[:, h * LANE:(h + 1) * LANE]).astype(BF16)


def _proj(x2, cos_t, sin_t, tri, w, tiles_per_seq):
    n = x2.shape[0]
    tm = PROJ_TM
    n_tab = cos_t.shape[0] // tm
    n_seq = n // (tiles_per_seq * tm)
    s_len = tiles_per_seq * tm
    row = lambda width: pl.BlockSpec((tm, width), lambda i: (i, 0))
    tab = pl.BlockSpec((tm, LANE), lambda i: (i % n_tab, 0))
    outs = [('t', MLA_ROW, F32), ('r', MLA_HEADS * SLAB, BF16), ('r', MLA_HEADS * NOPE, BF16),
            ('t', MLA_HEADS * MLA_V, BF16), ('r', LANE, BF16), ('r', FOX_HEADS * SLAB, BF16),
            ('t', FOX_KV * FOX_DIM, F32), ('r', FOX_KV * FOX_DIM, BF16), ('t', FOX_KV * FOX_DIM, F32),
            ('t', FOX_KV * FOX_DIM, BF16), ('t', FOX_HEADS, F32), ('r', LANE, F32), ('r', LANE, BF16)]
    out_specs, out_shape = [], []
    for kind, width, dtype in outs:
        if kind == 'r':
            out_specs.append(row(width))
            out_shape.append(jax.ShapeDtypeStruct((n, width), dtype))
        else:
            out_specs.append(pl.BlockSpec((width, tm), lambda i: (i // tiles_per_seq, i % tiles_per_seq)))
            out_shape.append(jax.ShapeDtypeStruct((n_seq * width, s_len), dtype))
    consts = [w['g_mix'], w['w_in'], w['g_q_lat'], w['w_q'], w['g_kv_lat'], w['w_kv'], w['g128'], w['place'],
              w['base_q']]
    return pl.pallas_call(
        functools.partial(_proj_kernel, tiles_per_seq=tiles_per_seq),
        grid=(n // tm,),
        in_specs=[row(D_MODEL), tab, tab, _const_spec(tri.shape)] + [_const_spec(c.shape) for c in consts],
        out_specs=out_specs,
        out_shape=out_shape,
        scratch_shapes=[pltpu.VMEM((8, LANE), F32)],
        compiler_params=pltpu.CompilerParams(dimension_semantics=("arbitrary",), vmem_limit_bytes=VMEM_LIMIT),
        name="proj",
    )(x2, cos_t, sin_t, tri, *consts)


def _attn_kernel(q_ref, km_ref, ke_ref, vt_ref, o_ref, *, hb, t, s_len, combos):
    cols = hb * t
    key_i = lax.broadcasted_iota(jnp.int32, (t, cols), 0)
    causal = key_i <= lax.broadcasted_iota(jnp.int32, (t, cols), 1) % t
    lo = lax.broadcasted_iota(jnp.int32, (LANE, 1), 0) < 64
    ones = jnp.ones((ONES_ROWS, t), BF16)

    def q_body(qb, _):
        r0 = pl.multiple_of(qb * t, t)
        qt = jnp.concatenate([q_ref[pl.ds(r0, t), h * SLAB:(h + 1) * SLAB].astype(F32).T for h in range(hb)],
                             axis=1).astype(BF16)

        def scores(c0):
            kc = jnp.concatenate([km_ref[pl.ds(c0, t), :], ke_ref[pl.ds(c0, t), :]], axis=1)
            return _dot(kc, qt)

        def update(s, c0, m, acc, masked):
            if masked:
                s = jnp.where(causal, s, NEG)
            m_new = jnp.maximum(m, jnp.max(s, axis=0, keepdims=True))
            alpha = jnp.exp(m - m_new)
            p = jnp.exp(s - m_new).astype(BF16)
            vt1 = jnp.concatenate([vt_ref[:, pl.ds(c0, t)], ones], axis=0)
            return m_new, alpha * acc + _dot(vt1, p)

        def body(kb, carry):
            s, m, acc = carry
            c0 = pl.multiple_of(kb * t, t)
            s_next = scores(pl.multiple_of(c0 + t, t))
            m, acc = update(s, c0, m, acc, False)
            return s_next, m, acc

        init = (scores(0), jnp.full((1, cols), NEG, F32), jnp.zeros((LANE + ONES_ROWS, cols), F32))
        s, m, acc = lax.fori_loop(0, qb, body, init)
        _, acc = update(s, r0, m, acc, True)
        o = acc[0:LANE] * (1.0 / acc[LANE:LANE + 1])
        out = [jnp.where(lo, o[:, a * t:(a + 1) * t], o[:, b * t:(b + 1) * t]).T for a, b in combos]
        o_ref[pl.ds(r0, t), :] = jnp.concatenate(out, axis=1).astype(o_ref.dtype)
        return 0

    lax.fori_loop(0, s_len // t, q_body, 0)


def _attention(q, km, ke, vt, batch, s_len, hb, combos, name):
    n_pairs = km.shape[1] // LANE
    ow = len(combos) * LANE
    return pl.pallas_call(
        functools.partial(_attn_kernel, hb=hb, t=ATTN_T, s_len=s_len, combos=combos),
        grid=(batch, n_pairs),
        in_specs=[pl.BlockSpec((s_len, hb * SLAB), lambda b, p: (b, p)),
                  pl.BlockSpec((s_len, LANE), lambda b, p: (b, p)),
                  pl.BlockSpec((s_len, LANE), lambda b, p: (b, 0)),
                  pl.BlockSpec((LANE, s_len), lambda b, p: (b * n_pairs + p, 0))],
        out_specs=pl.BlockSpec((s_len, ow), lambda b, p: (b, p)),
        out_shape=jax.ShapeDtypeStruct((batch * s_len, n_pairs * ow), BF16),
        compiler_params=pltpu.CompilerParams(dimension_semantics=("arbitrary", "arbitrary"),
                                             vmem_limit_bytes=VMEM_LIMIT),
        name=name,
    )(q, km, ke, vt)


def _memkv_kernel(mem_ref, gmem_ref, w_ref, gmk_ref, kt_ref, vt_ref):
    mn = (_rms(mem_ref[...], D_MODEL) * gmem_ref[...]).astype(BF16)
    kv = _dot(mn, w_ref[...])
    k = _group64_rms(kv[:, 0:MEM_W]) * gmk_ref[...]
    kt_ref[0] = k.T
    vt_ref[0] = kv[:, MEM_W:].T


def _memkv(mem2, batch, mem_len, w):
    return pl.pallas_call(
        _memkv_kernel,
        grid=(batch,),
        in_specs=[pl.BlockSpec((mem_len, D_MODEL), lambda b: (b, 0)), _const_spec(w['g_mem'].shape),
                  _const_spec(w['w_mkv'].shape), _const_spec(w['g_mk'].shape)],
        out_specs=[pl.BlockSpec((1, MEM_W, mem_len), lambda b: (b, 0, 0))] * 2,
        out_shape=[jax.ShapeDtypeStruct((batch, MEM_W, mem_len), F32)] * 2,
        compiler_params=pltpu.CompilerParams(dimension_semantics=("arbitrary",), vmem_limit_bytes=VMEM_LIMIT),
        name="memkv",
    )(mem2, w['g_mem'], w['w_mkv'], w['g_mk'])


def _finish_kernel(x_ref, om_ref, mk_ref, mv_ref, wo_ref, gcross_ref, wmq_ref, gmq_ref, wmo_ref, gffn_ref,
                   wup_ref, wdown_ref, y_ref, q_scr, o_scr, *, spt, ts):
    h = x_ref[...] + _dot(om_ref[...].astype(BF16), wo_ref[...])
    hn = (_rms(h, D_MODEL) * gcross_ref[...]).astype(BF16)
    q_scr[...] = _group64_rms(_dot(hn, wmq_ref[...])) * gmq_ref[...]
    head_of_lane = lax.broadcasted_iota(jnp.int32, (1, MEM_W), 1) // MEM_DIM

    def one_seq(s, _):
        r0 = pl.multiple_of(s * ts, ts)
        q = q_scr[pl.ds(r0, ts), :]
        qs = jnp.concatenate([jnp.where(head_of_lane == hd, q, 0.0) for hd in range(MEM_HEADS)], axis=0)
        sc = _dot(qs.astype(BF16), mk_ref[s].astype(BF16))
        p = jnp.exp(sc - jnp.max(sc, axis=1, keepdims=True))
        l = jnp.sum(p, axis=1, keepdims=True)
        pv = _dot_nt(p.astype(BF16), mv_ref[s].astype(BF16)) * (1.0 / l)
        o = jnp.where(head_of_lane == 0, pv[0:ts], 0.0)
        for hd in range(1, MEM_HEADS):
            o = o + jnp.where(head_of_lane == hd, pv[hd * ts:(hd + 1) * ts], 0.0)
        o_scr[pl.ds(r0, ts), :] = o
        return 0

    if spt == 1:
        one_seq(0, 0)
    else:
        lax.fori_loop(0, spt, one_seq, 0)

    h = h + _dot(o_scr[...].astype(BF16), wmo_ref[...])
    hn = (_rms(h, D_MODEL) * gffn_ref[...]).astype(BF16)
    y = h
    for c in range(D_FF // FFN_CHUNK):
        u = jnp.maximum(_dot(hn, wup_ref[:, c * FFN_CHUNK:(c + 1) * FFN_CHUNK]), 0.0)
        y = y + _dot((u * u).astype(BF16), wdown_ref[c * FFN_CHUNK:(c + 1) * FFN_CHUNK, :])
    y_ref[...] = y


def _finish(x2, om2, mk, mv, w_o, w, spt, ts, tiles_per_seq, name):
    assert spt == 1 or tiles_per_seq == 1
    n = x2.shape[0]
    tm = spt * ts
    mem_len = mk.shape[2]
    mem_spec = pl.BlockSpec((spt, MEM_W, mem_len), lambda i: (i // tiles_per_seq, 0, 0))
    consts = [w_o, w['g_cross'], w['w_mq'], w['g_mq'], w['w_mo'], w['g_ffn'], w['w_up'], w['w_down']]
    return pl.pallas_call(
        functools.partial(_finish_kernel, spt=spt, ts=ts),
        grid=(n // tm,),
        in_specs=[pl.BlockSpec((tm, D_MODEL), lambda i: (i, 0)), pl.BlockSpec((tm, D_MODEL), lambda i: (i, 0)),
                  mem_spec, mem_spec] + [_const_spec(c.shape) for c in consts],
        out_specs=pl.BlockSpec((tm, D_MODEL), lambda i: (i, 0)),
        out_shape=jax.ShapeDtypeStruct((n, D_MODEL), F32),
        scratch_shapes=[pltpu.VMEM((tm, MEM_W), F32), pltpu.VMEM((tm, MEM_W), F32)],
        compiler_params=pltpu.CompilerParams(dimension_semantics=("arbitrary",), vmem_limit_bytes=VMEM_LIMIT),
        name=name,
    )(x2, om2, mk, mv, *consts)


def _decode_kernel(pt_ref, qm_ref, fq_ref, cum_ref, locm_ref, lock_ref, locv_ref, locf_ref,
                   wkt_ref, wv_ref, gkn_ref, u_ref, linc_ref, cm_hbm, ck_hbm, cv_hbm, cl_hbm, o_ref,
                   mla_buf, fk_buf, fv_buf, lf_buf, sems, lhs_s, qr_s, qbd_s, carry_s, m1, l1, ctx, m2, l2, acc2,
                   *pend, nj, n_pages, n_seq):
    b = pl.program_id(0)
    j = pl.program_id(1)
    step = b * nj + j
    nrow = MLA_HEADS * DEC_T
    pend_a, pend_b = pend[0:4], pend[4:8]

    def page_copies(seq, jj, to_slot):
        base = seq * n_pages + (nj - 1 - jj) * PPS
        copies = []
        for i in range(PPS):
            page = pt_ref[base + i]
            for k, (src, dst) in enumerate(((cm_hbm, mla_buf), (ck_hbm, fk_buf), (cv_hbm, fv_buf), (cl_hbm, lf_buf))):
                copies.append(pltpu.make_async_copy(src.at[page], dst.at[to_slot, i], sems.at[to_slot, k]))
        return copies

    @pl.when(step == 0)
    def _():
        for c in page_copies(0, 0, 0):
            c.start()

    @pl.when(step + 1 < n_seq * nj)
    def _():
        wrap = j == nj - 1
        for c in page_copies(jnp.where(wrap, b + 1, b), jnp.where(wrap, 0, j + 1), (step + 1) % 2):
            c.start()

    for c in page_copies(b, j, step % 2):
        c.wait()
    cum8 = cum_ref[0]

    def mla_scores(lat_t, rope_t):
        big = _dot(lhs_s[...], lat_t)
        rs = _dot(qr_s[...], rope_t)
        slabs = []
        for h in range(MLA_HEADS):
            kvh = big[h * NOPE:(h + 1) * NOPE]
            r = lax.rsqrt(jnp.sum(kvh * kvh, axis=0, keepdims=True) * (1.0 / NOPE) + EPS)
            base = MLA_HEADS * NOPE + h * DEC_T
            slabs.append(big[base:base + DEC_T] * r + rs[h * DEC_T:(h + 1) * DEC_T])
        return jnp.concatenate(slabs, axis=0)

    def fox_scores(k_t, bias8):
        s = _dot(qbd_s[...], k_t)
        slabs = []
        for h in range(FOX_HEADS):
            slabs.append(s[h * DEC_T:(h + 1) * DEC_T] + bias8[h:h + 1, :] + cum8[:, h:h + 1])
        return jnp.concatenate(slabs, axis=0)

    def update(s, m_ref, l_ref, acc_ref, vals_t):
        m_old = m_ref[...]
        m_new = jnp.maximum(m_old, jnp.max(s, axis=1, keepdims=True))
        alpha = jnp.exp(m_old - m_new)
        p = jnp.exp(s - m_new)
        l_ref[...] = alpha * l_ref[...] + jnp.sum(p, axis=1, keepdims=True)
        m_ref[...] = m_new
        acc_ref[...] = alpha * acc_ref[...] + _dot_nt(p.astype(BF16), vals_t)

    @pl.when(j == 0)
    def _():
        qm8 = qm_ref[0].astype(F32)
        qn = jnp.concatenate([qm8[:, (2 * p) * SLAB:(2 * p) * SLAB + LANE]
                              + qm8[:, (2 * p + 1) * SLAB:(2 * p + 1) * SLAB + LANE] for p in range(4)], axis=1)
        qn = qn * gkn_ref[...]
        head512 = lax.broadcasted_iota(jnp.int32, (1, MLA_HEADS * NOPE), 1) // NOPE
        qn_bd = jnp.concatenate([jnp.where(head512 == h, qn, 0.0) for h in range(MLA_HEADS)], axis=0)
        lhs_s[0:MLA_HEADS * NOPE, :] = wkt_ref[...]
        lhs_s[MLA_HEADS * NOPE:, :] = _dot(qn_bd.astype(BF16), wkt_ref[...]).astype(BF16)
        qr_s[...] = jnp.concatenate([qm8[:, h * SLAB + LANE:h * SLAB + LANE + ROPE]
                                     for h in range(MLA_HEADS)], axis=0).astype(BF16)
        fq8 = fq_ref[0].astype(F32)
        zero = jnp.zeros((DEC_T, LANE), F32)
        blocks = []
        for h in range(FOX_HEADS):
            slab = fq8[:, h * SLAB:h * SLAB + LANE]
            blocks.append(jnp.concatenate([slab, zero] if (h // 2) // 2 == 0 else [zero, slab], axis=1))
        qbd_s[...] = jnp.concatenate(blocks, axis=0).astype(BF16)
        carry_s[...] = jnp.zeros_like(carry_s)
        m1[...] = jnp.full(m1.shape, NEG, F32)
        m2[...] = jnp.full(m2.shape, NEG, F32)
        l1[...] = jnp.zeros_like(l1)
        l2[...] = jnp.zeros_like(l2)
        ctx[...] = jnp.zeros_like(ctx)
        acc2[...] = jnp.zeros_like(acc2)

        tok = lax.broadcasted_iota(jnp.int32, (nrow, LANE), 0) % DEC_T
        valid = lax.broadcasted_iota(jnp.int32, (nrow, LANE), 1) <= tok
        lat_t = locm_ref[0, 0:KV_LORA, :].astype(BF16)
        s1 = mla_scores(lat_t, locm_ref[0, KV_LORA:MLA_ROW, :].astype(BF16))
        cum_t = _dot3(locf_ref[0], linc_ref[...])
        s2 = fox_scores(lock_ref[0].astype(BF16), -cum_t)
        s1_p, s2_p, lat_p, v_p = pend_b
        s1_p[...] = jnp.full(s1_p.shape, NEG, F32)
        s2_p[...] = jnp.full(s2_p.shape, NEG, F32)
        lat_p[...] = jnp.zeros_like(lat_p)
        v_p[...] = jnp.zeros_like(v_p)
        s1_p[:, 0:LANE] = jnp.where(valid, s1, NEG)
        s2_p[:, 0:LANE] = jnp.where(valid, s2, NEG)
        lat_p[:, 0:LANE] = lat_t
        v_p[:, 0:LANE] = locv_ref[0].astype(BF16)

    def score_phase(sl, queue):
        s1_p, s2_p, lat_p, v_p = queue
        lat_t = jnp.concatenate([mla_buf[sl, i, 0:KV_LORA, :].astype(BF16) for i in range(PPS)], axis=1)
        rope_t = jnp.concatenate([mla_buf[sl, i, KV_LORA:MLA_ROW, :].astype(BF16) for i in range(PPS)], axis=1)
        s1_p[...] = mla_scores(lat_t, rope_t)
        lat_p[...] = lat_t
        lfs = jnp.concatenate([lf_buf[sl, i] for i in range(PPS)], axis=0)
        res = _dot3(lfs, u_ref[...])
        carry = carry_s[...]
        biases = [None] * PPS
        for i in reversed(range(PPS)):
            biases[i] = res[i * 8:(i + 1) * 8, 0:LANE] + carry
            carry = carry + res[i * 8:(i + 1) * 8, LANE:2 * LANE]
        carry_s[...] = carry
        k_t = jnp.concatenate([fk_buf[sl, i].astype(BF16) for i in range(PPS)], axis=1)
        s2_p[...] = fox_scores(k_t, jnp.concatenate(biases, axis=1))
        v_p[...] = jnp.concatenate([fv_buf[sl, i].astype(BF16) for i in range(PPS)], axis=1)

    def value_phase(queue):
        s1_p, s2_p, lat_p, v_p = queue
        update(s1_p[...], m1, l1, ctx, lat_p[...])
        update(s2_p[...], m2, l2, acc2, v_p[...])

    @pl.when(j % 2 == 0)
    def _():
        score_phase(0, pend_a)
        value_phase(pend_b)

    @pl.when(j % 2 == 1)
    def _():
        score_phase(1, pend_b)
        value_phase(pend_a)

    @pl.when(j == nj - 1)
    def _():
        value_phase(pend_b)
        full = _dot((ctx[...] * (1.0 / l1[...])).astype(BF16), wv_ref[...])
        head512 = lax.broadcasted_iota(jnp.int32, (1, MLA_HEADS * MLA_V), 1) // MLA_V
        o_mla = jnp.where(head512 == 0, full[0:DEC_T], 0.0)
        for h in range(1, MLA_HEADS):
            o_mla = o_mla + jnp.where(head512 == h, full[h * DEC_T:(h + 1) * DEC_T], 0.0)
        a2 = acc2[...] * (1.0 / l2[...])
        head256 = lax.broadcasted_iota(jnp.int32, (1, FOX_KV * FOX_DIM), 1) // FOX_DIM
        outs = []
        for grp in range(FOX_HEADS // FOX_KV):
            o = None
            for kvh in range(FOX_KV):
                r0 = (kvh * 2 + grp) * DEC_T
                term = jnp.where(head256 == kvh, a2[r0:r0 + DEC_T], 0.0)
                o = term if o is None else o + term
            outs.append(o)
        o_ref[0] = jnp.concatenate([o_mla] + outs, axis=1)


def _decode(page_table_flat, qm, fq, cum, locm, lock, locv, locf, cm, ck, cv, cl, w, n_seq, n_pages):
    nj = n_pages // PPS
    assert nj % 2 == 0, "buffer slots alternate with the step parity within a sequence"
    keys = PPS * LANE

    seq = lambda rows, width: pl.BlockSpec((1, rows, width), lambda b, j, pt: (b, 0, 0))
    cspec = lambda a: pl.BlockSpec(a.shape, lambda b, j, pt: (0,) * a.ndim)
    consts = [w['w_kt'], w['w_v'], w['g_kn512'], w['u_mat'], w['l_inc']]
    nrow = MLA_HEADS * DEC_T
    grid_spec = pltpu.PrefetchScalarGridSpec(
        num_scalar_prefetch=1,
        grid=(n_seq, nj),
        in_specs=[seq(DEC_T, MLA_HEADS * SLAB), seq(DEC_T, FOX_HEADS * SLAB), seq(DEC_T, LANE),
                  seq(MLA_ROW, LANE), seq(FOX_KV * FOX_DIM, LANE), seq(FOX_KV * FOX_DIM, LANE), seq(FOX_HEADS, LANE)]
        + [cspec(c) for c in consts] + [pl.BlockSpec(memory_space=pl.ANY)] * 4,
        out_specs=pl.BlockSpec((1, DEC_T, D_MODEL), lambda b, j, pt: (b, 0, 0)),
        scratch_shapes=[pltpu.VMEM((2, PPS, MLA_ROW, LANE), F32), pltpu.VMEM((2, PPS, FOX_KV * FOX_DIM, LANE), F32),
                        pltpu.VMEM((2, PPS, FOX_KV * FOX_DIM, LANE), F32), pltpu.VMEM((2, PPS, FOX_HEADS, LANE), F32),
                        pltpu.SemaphoreType.DMA((2, 4)),
                        pltpu.VMEM((MLA_HEADS * NOPE + nrow, KV_LORA), BF16), pltpu.VMEM((nrow, ROPE), BF16),
                        pltpu.VMEM((nrow, FOX_KV * FOX_DIM), BF16), pltpu.VMEM((FOX_HEADS, LANE), F32),
                        pltpu.VMEM((nrow, 1), F32), pltpu.VMEM((nrow, 1), F32), pltpu.VMEM((nrow, KV_LORA), F32),
                        pltpu.VMEM((nrow, 1), F32), pltpu.VMEM((nrow, 1), F32),
                        pltpu.VMEM((nrow, FOX_KV * FOX_DIM), F32)]
        + [pltpu.VMEM((nrow, keys), F32), pltpu.VMEM((nrow, keys), F32), pltpu.VMEM((KV_LORA, keys), BF16),
           pltpu.VMEM((FOX_KV * FOX_DIM, keys), BF16)] * 2)
    return pl.pallas_call(
        functools.partial(_decode_kernel, nj=nj, n_pages=n_pages, n_seq=n_seq),
        grid_spec=grid_spec,
        out_shape=jax.ShapeDtypeStruct((n_seq, DEC_T, D_MODEL), F32),
        compiler_params=pltpu.CompilerParams(dimension_semantics=("arbitrary", "arbitrary"),
                                             vmem_limit_bytes=VMEM_LIMIT),
        name="decode",
    )(page_table_flat, qm, fq, cum, locm, lock, locv, locf, *consts, cm, ck, cv, cl)


def _place(vec, offset, width=LANE):
    return jnp.zeros((width,), F32).at[offset:offset + vec.shape[0]].set(vec)


def _rot_half(v, axis=-1):
    a, b = jnp.split(v, 2, axis=axis)
    return jnp.concatenate([-b, a], axis=axis)


def _prep_weights(g_mix, w_in, g_q_lat, w_q_up, g_kv_lat, g_k_rope, g_q_nope, g_q_rope, g_k_nope, w_kv_up, g_fox_q,
                  g_fox_k, b_forget, w_o, g_cross, g_mem, w_mq, w_mk, w_mv, g_mq, g_mk, w_mo, g_ffn, w_up, w_down):
    w = {}
    win = w_in[0]
    seg = [win[:, IN_SPLITS[k]:IN_SPLITS[k + 1]] for k in range(7)]
    c_q, c_kv, k_r, f_q, f_k, f_v, f_l = seg
    zc = lambda n: jnp.zeros((D_MODEL, n), F32)
    cols = [c_q, c_kv, k_r, zc(LANE - ROPE), _rot_half(k_r), zc(LANE - ROPE)]
    for h in range(FOX_HEADS):
        off = ((h // 2) % 2) * FOX_DIM
        cols += [zc(off), f_q[:, h * FOX_DIM:(h + 1) * FOX_DIM], zc(LANE - FOX_DIM - off)]
    cols += [f_k, f_v, f_l, zc(LANE - FOX_HEADS)]
    w['w_in'] = jnp.concatenate([c for c in cols if c.shape[1]], axis=1).astype(BF16)

    wq = w_q_up[0]
    zq = lambda n: jnp.zeros((Q_LORA, n), F32)
    qcols, rcols = [], []
    for h in range(MLA_HEADS):
        base = h * (NOPE + ROPE)
        off = (h % 2) * NOPE
        rope_w = wq[:, base + NOPE:base + NOPE + ROPE]
        qcols += [zq(off), wq[:, base:base + NOPE], zq(LANE - NOPE - off), rope_w, zq(LANE - ROPE)]
        rcols += [_rot_half(rope_w), zq(LANE - ROPE)]
    w['w_q'] = jnp.concatenate([c for c in qcols + rcols if c.shape[1]], axis=1).astype(BF16)

    wkv = w_kv_up[0].reshape(KV_LORA, MLA_HEADS, NOPE + MLA_V)
    w_k = wkv[:, :, :NOPE].reshape(KV_LORA, MLA_HEADS * NOPE)
    w_v = wkv[:, :, NOPE:].reshape(KV_LORA, MLA_HEADS * MLA_V)
    w['w_kv'] = jnp.concatenate([w_k, w_v], axis=1).astype(BF16)
    w['w_kt'] = w_k.T.astype(BF16)
    w['w_v'] = w_v.astype(BF16)
    w['g_kn512'] = jnp.tile(g_k_nope[0], MLA_HEADS)[None, :]

    gqr, gkr = g_q_rope[0] * MLA_SCALE, g_k_rope[0]
    swap = lambda v: jnp.concatenate([v[HALF:], v[:HALF]])
    k_aug_base = jnp.zeros((LANE,), F32).at[0:3].set(1.0)
    rows = [_place(g_q_nope[0] * MLA_SCALE, 0), _place(g_q_nope[0] * MLA_SCALE, NOPE), _place(gqr, 0),
            _place(swap(gqr), 0), _place(gkr, 0), _place(swap(gkr), 0), jnp.tile(g_k_nope[0], 2),
            _place(g_fox_q[0] * FOX_SCALE, 0), _place(g_fox_q[0] * FOX_SCALE, FOX_DIM), jnp.tile(g_fox_k[0], 2),
            _place(b_forget[0], 0), k_aug_base]
    rows += [jnp.zeros((LANE,), F32)] * (16 - len(rows))
    w['g128'] = jnp.stack(rows)

    place = np.zeros((3, LANE, LANE + FOX_HEADS * LANE), np.float32)
    base_q = np.zeros((1, FOX_HEADS * LANE), np.float32)
    for piece in range(3):
        for h in range(FOX_HEADS):
            place[piece, h, 3 + 3 * h + piece] = -1.0
            place[piece, h, LANE + h * LANE + piece] = 1.0
            base_q[0, h * LANE + 3 + 3 * h + piece] = 1.0
    w['place'] = jnp.asarray(place, BF16)
    w['base_q'] = jnp.asarray(base_q)

    w['g_mix'], w['g_q_lat'], w['g_kv_lat'] = g_mix, g_q_lat, g_kv_lat
    w['g_cross'], w['g_mem'], w['g_ffn'] = g_cross, g_mem, g_ffn
    w['g_mq'] = jnp.tile(g_mq[0] * MEM_SCALE, MEM_HEADS)[None, :]
    w['g_mk'] = jnp.tile(g_mk[0], MEM_HEADS)[None, :]
    w['w_mkv'] = jnp.concatenate([w_mk[0], w_mv[0]], axis=1).astype(BF16)
    w['w_mq'] = w_mq[0].astype(BF16)
    w['w_mo'] = w_mo[0].astype(BF16)
    w['w_up'] = w_up[0].astype(BF16)
    w['w_down'] = w_down[0].astype(BF16)

    wo = w_o[0]
    mla_w = MLA_HEADS * MLA_V
    fox_rows = wo[mla_w:].reshape(FOX_KV // 2, 2, 2, FOX_DIM, D_MODEL)
    prompt_fox = jnp.transpose(fox_rows, (0, 2, 1, 3, 4)).reshape(FOX_HEADS * FOX_DIM, D_MODEL)
    w['w_o'] = jnp.concatenate([wo[:mla_w], prompt_fox], axis=0).astype(BF16)
    fox_rows = wo[mla_w:].reshape(FOX_KV, 2, FOX_DIM, D_MODEL)
    sample_fox = jnp.transpose(fox_rows, (1, 0, 2, 3)).reshape(FOX_HEADS * FOX_DIM, D_MODEL)
    w['w_o_sample'] = jnp.concatenate([wo[:mla_w], sample_fox], axis=0).astype(BF16)

    idx = np.arange(LANE)
    u_strict = (idx[:, None] > idx[None, :]).astype(np.float32)
    w['u_mat'] = jnp.asarray(np.concatenate([u_strict, np.ones((LANE, LANE), np.float32)], axis=1), BF16)
    w['l_inc'] = jnp.asarray((idx[:, None] <= idx[None, :]).astype(np.float32), BF16)
    return w


def _rope_tables(pos):
    inv_freq = ROPE_BASE ** (-jnp.arange(HALF, dtype=F32) / HALF)
    ang = pos.astype(F32)[:, None] * inv_freq[None, :]
    pad = jnp.zeros((pos.shape[0], LANE - ROPE), F32)
    cos = jnp.concatenate([jnp.cos(ang), jnp.cos(ang), pad], axis=1)
    sin = jnp.concatenate([jnp.sin(ang), jnp.sin(ang), pad], axis=1)
    return cos, sin


def _tri(tm, group):
    idx = np.arange(tm)
    m = (idx[None, :] <= idx[:, None]) & (idx[None, :] // group == idx[:, None] // group)
    return jnp.asarray(m.astype(np.float32), BF16)


def kernel(x_prompt, x_sample, cache_mla, cache_fox_k, cache_fox_v, cache_fox_logf, cache_mem_k, cache_mem_v,
           page_table, mem_prompt, g_mix, w_in, g_q_lat, w_q_up, g_kv_lat, g_k_rope, g_q_nope, g_q_rope, g_k_nope,
           w_kv_up, g_fox_q, g_fox_k, b_forget, w_o, g_cross, g_mem, w_mq, w_mk, w_mv, g_mq, g_mk, w_mo, g_ffn, w_up,
           w_down):
    assert cache_mla.shape[0] == 1, "single-layer kernel"
    batch, s_len, _ = x_prompt.shape
    n_seq, t_new, _ = x_sample.shape
    n_pool, page = cache_mla.shape[1], cache_mla.shape[2]
    n_pages = page_table.shape[1]
    mem_len = mem_prompt.shape[1]
    assert page == LANE and t_new <= DEC_T and n_pages % PPS == 0
    assert s_len % PROJ_TM == 0 and s_len % ATTN_T == 0 and s_len % FIN_TM == 0
    w = _prep_weights(g_mix, w_in, g_q_lat, w_q_up, g_kv_lat, g_k_rope, g_q_nope, g_q_rope, g_k_nope, w_kv_up,
                      g_fox_q, g_fox_k, b_forget, w_o, g_cross, g_mem, w_mq, w_mk, w_mv, g_mq, g_mk, w_mo, g_ffn,
                      w_up, w_down)

    n = batch * s_len
    xp = x_prompt.reshape(n, D_MODEL)
    cos_p, sin_p = _rope_tables(jnp.arange(s_len))
    (p_row_t, qm, kn, vm_t, kr, fq, fk_t, fkb, fv_t, fvb_t, p_lf_t, _, kaug) = _proj(
        xp, cos_p, sin_p, _tri(PROJ_TM, PROJ_TM), w, s_len // PROJ_TM)
    o_mla = _attention(qm, kn, kr, vm_t, batch, s_len, 2, ((0, 1),), "attn_mla")
    o_fox = _attention(fq, fkb, kaug, fvb_t, batch, s_len, 4, ((0, 2), (1, 3)), "attn_fox")
    mk_t, mv_t = _memkv(mem_prompt.reshape(batch * mem_len, D_MODEL), batch, mem_len, w)
    y_prompt = _finish(xp, jnp.concatenate([o_mla, o_fox], axis=1), mk_t, mv_t, w['w_o'], w, 1, FIN_TM,
                       s_len // FIN_TM, "finish_prompt")

    ns = n_seq * DEC_T
    xs = jnp.pad(x_sample, ((0, 0), (0, DEC_T - t_new), (0, 0))).reshape(ns, D_MODEL)
    pos_s = n_pages * page + (jnp.arange(PROJ_TM) % DEC_T)
    cos_s, sin_s = _rope_tables(pos_s)
    (s_row_t, qm_s, _, _, _, fq_s, fk_s_t, _, fv_s_t, _, s_lf_t, cum_s, _) = _proj(
        xs, cos_s, sin_s, _tri(PROJ_TM, DEC_T), w, 1)

    def new_tokens(a_t):
        feat = a_t.shape[0] // (ns // PROJ_TM)
        a = jnp.transpose(a_t.reshape(ns // PROJ_TM, feat, PROJ_TM), (1, 0, 2))
        return a.reshape(feat, n_seq, DEC_T)[:, :, :t_new]

    def new_page(a_t):
        return jnp.pad(jnp.transpose(new_tokens(a_t), (1, 0, 2)), ((0, 0), (0, 0), (0, LANE - t_new)))

    cm = jnp.transpose(cache_mla[0], (0, 2, 1))
    ck = jnp.transpose(cache_fox_k[0], (0, 2, 3, 1)).reshape(n_pool, FOX_KV * FOX_DIM, page)
    cv = jnp.transpose(cache_fox_v[0], (0, 2, 3, 1)).reshape(n_pool, FOX_KV * FOX_DIM, page)
    cl = jnp.transpose(cache_fox_logf[0], (0, 2, 1))
    o_s = _decode(page_table.reshape(-1), qm_s.reshape(n_seq, DEC_T, -1), fq_s.reshape(n_seq, DEC_T, -1),
                  cum_s.reshape(n_seq, DEC_T, LANE), new_page(s_row_t), new_page(fk_s_t), new_page(fv_s_t),
                  new_page(s_lf_t), cm, ck, cv, cl, w, n_seq, n_pages)
    cmk = jnp.transpose(cache_mem_k[0], (0, 2, 3, 1)).reshape(n_seq, MEM_W, mem_len)
    cmv = jnp.transpose(cache_mem_v[0], (0, 2, 3, 1)).reshape(n_seq, MEM_W, mem_len)
    y_s = _finish(xs, o_s.reshape(ns, D_MODEL), cmk, cmv, w['w_o_sample'], w, 16, DEC_T, 1, "finish_sample")

    def tok(a_t, *shape):
        return jnp.transpose(new_tokens(a_t), (1, 2, 0)).reshape(1, n_seq, t_new, *shape)

    def seq_out(a_t, *shape):
        return jnp.transpose(a_t.reshape(batch, -1, a_t.shape[1]), (0, 2, 1)).reshape(1, batch, a_t.shape[1], *shape)

    y_sample = y_s.reshape(n_seq, DEC_T, D_MODEL)[:, :t_new]
    return (y_prompt.reshape(batch, s_len, D_MODEL), y_sample,
            seq_out(p_row_t, MLA_ROW), seq_out(fk_t, FOX_KV, FOX_DIM), seq_out(fv_t, FOX_KV, FOX_DIM),
            seq_out(p_lf_t, FOX_HEADS), seq_out(mk_t.reshape(batch * MEM_W, mem_len), MEM_HEADS, MEM_DIM),
            seq_out(mv_t.reshape(batch * MEM_W, mem_len), MEM_HEADS, MEM_DIM),
            tok(s_row_t, MLA_ROW), tok(fk_s_t, FOX_KV, FOX_DIM), tok(fv_s_t, FOX_KV, FOX_DIM),
            tok(s_lf_t, FOX_HEADS))
```

```python
import functools

import numpy as np
import jax
import jax.numpy as jnp
from jax import lax
from jax.experimental import pallas as pl
from jax.experimental.pallas import tpu as pltpu

F32 = jnp.float32
BF16 = jnp.bfloat16

D_MODEL = 1024
Q_LORA = 384
KV_LORA = 256
ROPE = 32
HALF = ROPE // 2
NOPE = 64
MLA_V = 64
MLA_HEADS = 8
MLA_ROW = KV_LORA + ROPE
FOX_HEADS = 8
FOX_KV = 4
FOX_DIM = 64
MEM_HEADS = 4
MEM_DIM = 64
MEM_W = MEM_HEADS * MEM_DIM
D_FF = 4096
IN_SPLITS = (0, 384, 640, 672, 1184, 1440, 1696, 1704)
ROPE_BASE = 10000.0
EPS = 1e-6
NEG = -1e30
MLA_SCALE = (NOPE + ROPE) ** -0.5
FOX_SCALE = FOX_DIM ** -0.5
MEM_SCALE = MEM_DIM ** -0.5

LANE = 128
SLAB = 2 * LANE
VMEM_LIMIT = 56 * 1024 * 1024
PROJ_TM = 256
ATTN_T = 512
FIN_TM = 512
DEC_T = 8
PPS = 32
FFN_CHUNK = 1024
ONES_ROWS = 16

C_CQ, C_CKV, C_KR, C_KROT, C_FQ, C_FK, C_FV, C_FL, W_IN_EXP = 0, 384, 640, 768, 896, 1920, 2176, 2432, 2560
G_QN_E, G_QN_O, G_QR, G_QR_ROT, G_KR, G_KR_ROT, G_KN, G_FQ_E, G_FQ_O, G_FK, G_BF, G_KAUG = range(12)


def _const_spec(shape):
    n = len(shape)
    return pl.BlockSpec(shape, lambda *_: (0,) * n, pipeline_mode=pl.Buffered(1))


def _rms(x, width):
    return x * lax.rsqrt(jnp.sum(x * x, axis=-1, keepdims=True) * (1.0 / width) + EPS)


def _group64_rms(x):
    out = []
    lane = lax.broadcasted_iota(jnp.int32, (1, LANE), 1)
    lo = lane < 64
    for j in range(x.shape[1] // LANE):
        s = x[:, j * LANE:(j + 1) * LANE]
        s2 = s * s
        r_lo = lax.rsqrt(jnp.sum(jnp.where(lo, s2, 0.0), axis=-1, keepdims=True) * (1.0 / 64) + EPS)
        r_hi = lax.rsqrt(jnp.sum(jnp.where(lo, 0.0, s2), axis=-1, keepdims=True) * (1.0 / 64) + EPS)
        out.append(s * jnp.where(lo, r_lo, r_hi))
    return out[0] if len(out) == 1 else jnp.concatenate(out, axis=1)


def _split3(x):
    hi = x.astype(BF16)
    r1 = x - hi.astype(F32)
    mid = r1.astype(BF16)
    lo = (r1 - mid.astype(F32)).astype(BF16)
    return hi, mid, lo


def _dot(a, b):
    return jnp.dot(a, b, preferred_element_type=F32)


def _dot_nt(a, b):
    return lax.dot_general(a, b, (((1,), (1,)), ((), ())), preferred_element_type=F32)


def _dot3(x, w):
    hi, mid, lo = _split3(x)
    return _dot(hi, w) + _dot(mid, w) + _dot(lo, w)


def _proj_kernel(x_ref, cos_ref, sin_ref, tri_ref, gmix_ref, win_ref, gql_ref, wq_ref, gkv_ref, wkv_ref,
                 g128_ref, place_ref, baseq_ref,
                 rowt_ref, qm_ref, kn_ref, vmt_ref, kr_ref, fq_ref, fkt_ref, fkb_ref, fvt_ref, fvbt_ref,
                 lft_ref, cum_ref, kaug_ref, carry_ref, *, tiles_per_seq):
    i = pl.program_id(0)
    cos = cos_ref[...]
    sin = sin_ref[...]

    def g(r):
        return g128_ref[r:r + 1, :]

    xn = (_rms(x_ref[...], D_MODEL) * gmix_ref[...]).astype(BF16)
    a = _dot(xn, win_ref[...])

    cq = (_rms(a[:, C_CQ:C_CQ + Q_LORA], Q_LORA) * gql_ref[...]).astype(BF16)
    q = _dot(cq, wq_ref[...])
    for h in range(MLA_HEADS):
        nope = q[:, h * SLAB:h * SLAB + LANE]
        rope = q[:, h * SLAB + LANE:(h + 1) * SLAB]
        rot = q[:, MLA_HEADS * SLAB + h * LANE:MLA_HEADS * SLAB + (h + 1) * LANE]
        rn = lax.rsqrt(jnp.sum(nope * nope, axis=-1, keepdims=True) * (1.0 / NOPE) + EPS)
        rr = lax.rsqrt(jnp.sum(rope * rope, axis=-1, keepdims=True) * (1.0 / ROPE) + EPS)
        qm_ref[:, h * SLAB:h * SLAB + LANE] = (nope * rn * g(G_QN_O if h % 2 else G_QN_E)).astype(BF16)
        qm_ref[:, h * SLAB + LANE:(h + 1) * SLAB] = (
            rr * (rope * g(G_QR) * cos + rot * g(G_QR_ROT) * sin)).astype(BF16)

    lat = _rms(a[:, C_CKV:C_CKV + KV_LORA], KV_LORA) * gkv_ref[...]
    kr = a[:, C_KR:C_KR + LANE]
    krot = a[:, C_KROT:C_KROT + LANE]
    rk = lax.rsqrt(jnp.sum(kr * kr, axis=-1, keepdims=True) * (1.0 / ROPE) + EPS)
    k_rope = rk * (kr * g(G_KR) * cos + krot * g(G_KR_ROT) * sin)
    rowt_ref[0:KV_LORA, :] = lat.T
    rowt_ref[KV_LORA:MLA_ROW, :] = k_rope.T[0:ROPE]
    kr_ref[...] = k_rope.astype(BF16)

    kv = _dot(lat.astype(BF16), wkv_ref[...])
    gkn = g(G_KN)
    kn = _group64_rms(kv[:, 0:MLA_HEADS * NOPE])
    kn_ref[...] = (kn * jnp.concatenate([gkn] * 4, axis=1)).astype(BF16)
    vmt_ref[...] = kv[:, MLA_HEADS * NOPE:].T.astype(BF16)

    for h in range(FOX_HEADS):
        s = a[:, C_FQ + h * LANE:C_FQ + (h + 1) * LANE]
        r = lax.rsqrt(jnp.sum(s * s, axis=-1, keepdims=True) * (1.0 / FOX_DIM) + EPS)
        fq_ref[:, h * SLAB:h * SLAB + LANE] = (s * r * g(G_FQ_O if (h // 2) % 2 else G_FQ_E)).astype(BF16)
    fk = _group64_rms(a[:, C_FK:C_FK + 2 * LANE]) * jnp.concatenate([g(G_FK)] * 2, axis=1)
    fkt_ref[...] = fk.T
    fkb_ref[...] = fk.astype(BF16)
    fvt = a[:, C_FV:C_FV + 2 * LANE].T
    fvt_ref[...] = fvt
    fvbt_ref[...] = fvt.astype(BF16)

    z = a[:, C_FL:C_FL + LANE] + g(G_BF)
    lane = lax.broadcasted_iota(jnp.int32, (1, LANE), 1)
    logf = jnp.where(lane < FOX_HEADS, jnp.minimum(z, 0.0) - jnp.log1p(jnp.exp(-jnp.abs(z))), 0.0)
    lft_ref[...] = logf.T[0:FOX_HEADS]

    @pl.when(i % tiles_per_seq == 0)
    def _():
        carry_ref[...] = jnp.zeros_like(carry_ref)

    l_hi, l_mid, l_lo = _split3(logf)
    tri = tri_ref[...]
    cum = _dot(tri, l_hi) + _dot(tri, l_mid) + _dot(tri, l_lo) + carry_ref[0:1, :]
    cum_ref[...] = cum
    tm = cum.shape[0]
    carry_ref[...] = jnp.broadcast_to(cum[tm - 1:tm, :], carry_ref.shape)

    c_hi, c_mid, c_lo = _split3(cum)
    aug = _dot(c_hi, place_ref[0]) + _dot(c_mid, place_ref[1]) + _dot(c_lo, place_ref[2])
    kaug_ref[...] = (aug[:, 0:LANE] + g(G_KAUG)).astype(BF16)
    for h in range(FOX_HEADS):
        fq_ref[:, h * SLAB + LANE:(h + 1) * SLAB] = (
            aug[:, (h + 1) * LANE:(h + 2) * LANE] + baseq_ref[:, h * LANE:(h + 1) * LANE]).astype(BF16)


def _proj(x2, cos_t, sin_t, tri, w, tiles_per_seq):
    n = x2.shape[0]
    tm = PROJ_TM
    n_tab = cos_t.shape[0] // tm
    n_seq = n // (tiles_per_seq * tm)
    s_len = tiles_per_seq * tm
    row = lambda width: pl.BlockSpec((tm, width), lambda i: (i, 0))
    tab = pl.BlockSpec((tm, LANE), lambda i: (i % n_tab, 0))
    outs = [('t', MLA_ROW, F32), ('r', MLA_HEADS * SLAB, BF16), ('r', MLA_HEADS * NOPE, BF16),
            ('t', MLA_HEADS * MLA_V, BF16), ('r', LANE, BF16), ('r', FOX_HEADS * SLAB, BF16),
            ('t', FOX_KV * FOX_DIM, F32), ('r', FOX_KV * FOX_DIM, BF16), ('t', FOX_KV * FOX_DIM, F32),
            ('t', FOX_KV * FOX_DIM, BF16), ('t', FOX_HEADS, F32), ('r', LANE, F32), ('r', LANE, BF16)]
    out_specs, out_shape = [], []
    for kind, width, dtype in outs:
        if kind == 'r':
            out_specs.append(row(width))
            out_shape.append(jax.ShapeDtypeStruct((n, width), dtype))
        else:
            out_specs.append(pl.BlockSpec((width, tm), lambda i: (i // tiles_per_seq, i % tiles_per_seq)))
            out_shape.append(jax.ShapeDtypeStruct((n_seq * width, s_len), dtype))
    consts = [w['g_mix'], w['w_in'], w['g_q_lat'], w['w_q'], w['g_kv_lat'], w['w_kv'], w['g128'], w['place'],
              w['base_q']]
    return pl.pallas_call(
        functools.partial(_proj_kernel, tiles_per_seq=tiles_per_seq),
        grid=(n // tm,),
        in_specs=[row(D_MODEL), tab, tab, _const_spec(tri.shape)] + [_const_spec(c.shape) for c in consts],
        out_specs=out_specs,
        out_shape=out_shape,
        scratch_shapes=[pltpu.VMEM((8, LANE), F32)],
        compiler_params=pltpu.CompilerParams(dimension_semantics=("arbitrary",), vmem_limit_bytes=VMEM_LIMIT),
        name="proj",
    )(x2, cos_t, sin_t, tri, *consts)


def _attn_kernel(q_ref, km_ref, ke_ref, vt_ref, o_ref, qt_s, s_a, s_b, m_s, acc_s, *, hb, t, s_len, combos):
    cols = hb * t
    key_i = lax.broadcasted_iota(jnp.int32, (t, cols), 0)
    causal = key_i <= lax.broadcasted_iota(jnp.int32, (t, cols), 1) % t
    lo = lax.broadcasted_iota(jnp.int32, (LANE, 1), 0) < 64
    ones = jnp.ones((ONES_ROWS, t), BF16)

    def scores(c0):
        kc = jnp.concatenate([km_ref[pl.ds(c0, t), :], ke_ref[pl.ds(c0, t), :]], axis=1)
        return _dot(kc, qt_s[...])

    def update(s_ref, c0, masked):
        s = s_ref[...]
        if masked:
            s = jnp.where(causal, s, NEG)
        m = m_s[...]
        m_new = jnp.maximum(m, jnp.max(s, axis=0, keepdims=True))
        alpha = jnp.exp(m - m_new)
        p = jnp.exp(s - m_new).astype(BF16)
        vt1 = jnp.concatenate([vt_ref[:, pl.ds(c0, t)], ones], axis=0)
        m_s[...] = m_new
        acc_s[...] = alpha * acc_s[...] + _dot(vt1, p)

    def q_body(qb, _):
        r0 = pl.multiple_of(qb * t, t)
        qt_s[...] = jnp.concatenate([q_ref[pl.ds(r0, t), h * SLAB:(h + 1) * SLAB].astype(F32).T for h in range(hb)],
                                    axis=1).astype(BF16)
        m_s[...] = jnp.full(m_s.shape, NEG, F32)
        acc_s[...] = jnp.zeros_like(acc_s)
        s_a[...] = scores(0)

        def two_blocks(i, _):
            c0 = pl.multiple_of(2 * i * t, t)
            s_b[...] = scores(pl.multiple_of(c0 + t, t))
            update(s_a, c0, False)
            s_a[...] = scores(pl.multiple_of(c0 + 2 * t, t))
            update(s_b, pl.multiple_of(c0 + t, t), False)
            return 0

        lax.fori_loop(0, qb // 2, two_blocks, 0)

        @pl.when(qb % 2 == 0)
        def _():
            update(s_a, r0, True)

        @pl.when(qb % 2 == 1)
        def _():
            s_b[...] = scores(r0)
            update(s_a, pl.multiple_of(r0 - t, t), False)
            update(s_b, r0, True)

        acc = acc_s[...]
        o = acc[0:LANE] * (1.0 / acc[LANE:LANE + 1])
        out = [jnp.where(lo, o[:, a * t:(a + 1) * t], o[:, b * t:(b + 1) * t]).T for a, b in combos]
        o_ref[pl.ds(r0, t), :] = jnp.concatenate(out, axis=1).astype(o_ref.dtype)
        return 0

    lax.fori_loop(0, s_len // t, q_body, 0)


def _attention(q, km, ke, vt, batch, s_len, hb, combos, name):
    n_pairs = km.shape[1] // LANE
    ow = len(combos) * LANE
    cols = hb * ATTN_T
    return pl.pallas_call(
        functools.partial(_attn_kernel, hb=hb, t=ATTN_T, s_len=s_len, combos=combos),
        grid=(batch, n_pairs),
        in_specs=[pl.BlockSpec((s_len, hb * SLAB), lambda b, p: (b, p)),
                  pl.BlockSpec((s_len, LANE), lambda b, p: (b, p)),
                  pl.BlockSpec((s_len, LANE), lambda b, p: (b, 0)),
                  pl.BlockSpec((LANE, s_len), lambda b, p: (b * n_pairs + p, 0))],
        out_specs=pl.BlockSpec((s_len, ow), lambda b, p: (b, p)),
        out_shape=jax.ShapeDtypeStruct((batch * s_len, n_pairs * ow), BF16),
        scratch_shapes=[pltpu.VMEM((SLAB, cols), BF16), pltpu.VMEM((ATTN_T, cols), F32),
                        pltpu.VMEM((ATTN_T, cols), F32), pltpu.VMEM((1, cols), F32),
                        pltpu.VMEM((LANE + ONES_ROWS, cols), F32)],
        compiler_params=pltpu.CompilerParams(dimension_semantics=("arbitrary", "arbitrary"),
                                             vmem_limit_bytes=VMEM_LIMIT),
        name=name,
    )(q, km, ke, vt)


def _memkv_kernel(mem_ref, gmem_ref, w_ref, gmk_ref, kt_ref, vt_ref):
    mn = (_rms(mem_ref[...], D_MODEL) * gmem_ref[...]).astype(BF16)
    kv = _dot(mn, w_ref[...])
    k = _group64_rms(kv[:, 0:MEM_W]) * gmk_ref[...]
    kt_ref[0] = k.T
    vt_ref[0] = kv[:, MEM_W:].T


def _memkv(mem2, batch, mem_len, w):
    return pl.pallas_call(
        _memkv_kernel,
        grid=(batch,),
        in_specs=[pl.BlockSpec((mem_len, D_MODEL), lambda b: (b, 0)), _const_spec(w['g_mem'].shape),
                  _const_spec(w['w_mkv'].shape), _const_spec(w['g_mk'].shape)],
        out_specs=[pl.BlockSpec((1, MEM_W, mem_len), lambda b: (b, 0, 0))] * 2,
        out_shape=[jax.ShapeDtypeStruct((batch, MEM_W, mem_len), F32)] * 2,
        compiler_params=pltpu.CompilerParams(dimension_semantics=("arbitrary",), vmem_limit_bytes=VMEM_LIMIT),
        name="memkv",
    )(mem2, w['g_mem'], w['w_mkv'], w['g_mk'])


def _finish_kernel(x_ref, om_ref, mk_ref, mv_ref, wo_ref, gcross_ref, wmq_ref, gmq_ref, wmo_ref, gffn_ref,
                   wup_ref, wdown_ref, y_ref, q_scr, o_scr, *, spt, ts):
    h = x_ref[...] + _dot(om_ref[...].astype(BF16), wo_ref[...])
    hn = (_rms(h, D_MODEL) * gcross_ref[...]).astype(BF16)
    q_scr[...] = _group64_rms(_dot(hn, wmq_ref[...])) * gmq_ref[...]
    head_of_lane = lax.broadcasted_iota(jnp.int32, (1, MEM_W), 1) // MEM_DIM

    def one_seq(s, _):
        r0 = pl.multiple_of(s * ts, ts)
        q = q_scr[pl.ds(r0, ts), :]
        qs = jnp.concatenate([jnp.where(head_of_lane == hd, q, 0.0) for hd in range(MEM_HEADS)], axis=0)
        sc = _dot(qs.astype(BF16), mk_ref[s].astype(BF16))
        p = jnp.exp(sc - jnp.max(sc, axis=1, keepdims=True))
        l = jnp.sum(p, axis=1, keepdims=True)
        pv = _dot_nt(p.astype(BF16), mv_ref[s].astype(BF16)) * (1.0 / l)
        o = jnp.where(head_of_lane == 0, pv[0:ts], 0.0)
        for hd in range(1, MEM_HEADS):
            o = o + jnp.where(head_of_lane == hd, pv[hd * ts:(hd + 1) * ts], 0.0)
        o_scr[pl.ds(r0, ts), :] = o
        return 0

    if spt == 1:
        one_seq(0, 0)
    else:
        lax.fori_loop(0, spt, one_seq, 0)

    h = h + _dot(o_scr[...].astype(BF16), wmo_ref[...])
    hn = (_rms(h, D_MODEL) * gffn_ref[...]).astype(BF16)
    y = h
    for c in range(D_FF // FFN_CHUNK):
        u = jnp.maximum(_dot(hn, wup_ref[:, c * FFN_CHUNK:(c + 1) * FFN_CHUNK]), 0.0)
        y = y + _dot((u * u).astype(BF16), wdown_ref[c * FFN_CHUNK:(c + 1) * FFN_CHUNK, :])
    y_ref[...] = y


def _finish(x2, om2, mk, mv, w_o, w, spt, ts, tiles_per_seq, name):
    assert spt == 1 or tiles_per_seq == 1
    n = x2.shape[0]
    tm = spt * ts
    mem_len = mk.shape[2]
    mem_spec = pl.BlockSpec((spt, MEM_W, mem_len), lambda i: (i // tiles_per_seq, 0, 0))
    consts = [w_o, w['g_cross'], w['w_mq'], w['g_mq'], w['w_mo'], w['g_ffn'], w['w_up'], w['w_down']]
    return pl.pallas_call(
        functools.partial(_finish_kernel, spt=spt, ts=ts),
        grid=(n // tm,),
        in_specs=[pl.BlockSpec((tm, D_MODEL), lambda i: (i, 0)), pl.BlockSpec((tm, D_MODEL), lambda i: (i, 0)),
                  mem_spec, mem_spec] + [_const_spec(c.shape) for c in consts],
        out_specs=pl.BlockSpec((tm, D_MODEL), lambda i: (i, 0)),
        out_shape=jax.ShapeDtypeStruct((n, D_MODEL), F32),
        scratch_shapes=[pltpu.VMEM((tm, MEM_W), F32), pltpu.VMEM((tm, MEM_W), F32)],
        compiler_params=pltpu.CompilerParams(dimension_semantics=("arbitrary",), vmem_limit_bytes=VMEM_LIMIT),
        name=name,
    )(x2, om2, mk, mv, *consts)


def _decode_kernel(pt_ref, qm_ref, fq_ref, cum_ref, locm_ref, lock_ref, locv_ref, locf_ref,
                   wkt_ref, wv_ref, gkn_ref, u_ref, linc_ref, cm_hbm, ck_hbm, cv_hbm, cl_hbm, o_ref,
                   mla_buf, fk_buf, fv_buf, lf_buf, sems, lhs_s, qr_s, qbd_s, carry_s, m1, l1, ctx, m2, l2, acc2,
                   *pend, nj, n_pages, n_seq):
    b = pl.program_id(0)
    j = pl.program_id(1)
    step = b * nj + j
    nrow = MLA_HEADS * DEC_T
    pend_a, pend_b = pend[0:4], pend[4:8]

    def page_copies(seq, jj, to_slot):
        base = seq * n_pages + (nj - 1 - jj) * PPS
        copies = []
        for i in range(PPS):
            page = pt_ref[base + i]
            for k, (src, dst) in enumerate(((cm_hbm, mla_buf), (ck_hbm, fk_buf), (cv_hbm, fv_buf), (cl_hbm, lf_buf))):
                copies.append(pltpu.make_async_copy(src.at[page], dst.at[to_slot, i], sems.at[to_slot, k]))
        return copies

    @pl.when(step == 0)
    def _():
        for c in page_copies(0, 0, 0):
            c.start()

    @pl.when(step + 1 < n_seq * nj)
    def _():
        wrap = j == nj - 1
        for c in page_copies(jnp.where(wrap, b + 1, b), jnp.where(wrap, 0, j + 1), (step + 1) % 2):
            c.start()

    for c in page_copies(b, j, step % 2):
        c.wait()
    cum8 = cum_ref[0]

    def mla_scores(lat_t, rope_t):
        big = _dot(lhs_s[...], lat_t)
        rs = _dot(qr_s[...], rope_t)
        slabs = []
        for h in range(MLA_HEADS):
            kvh = big[h * NOPE:(h + 1) * NOPE]
            r = lax.rsqrt(jnp.sum(kvh * kvh, axis=0, keepdims=True) * (1.0 / NOPE) + EPS)
            base = MLA_HEADS * NOPE + h * DEC_T
            slabs.append(big[base:base + DEC_T] * r + rs[h * DEC_T:(h + 1) * DEC_T])
        return jnp.concatenate(slabs, axis=0)

    def fox_scores(k_t, bias8):
        s = _dot(qbd_s[...], k_t)
        slabs = []
        for h in range(FOX_HEADS):
            slabs.append(s[h * DEC_T:(h + 1) * DEC_T] + bias8[h:h + 1, :] + cum8[:, h:h + 1])
        return jnp.concatenate(slabs, axis=0)

    def update(s, m_ref, l_ref, acc_ref, vals_t):
        m_old = m_ref[...]
        m_new = jnp.maximum(m_old, jnp.max(s, axis=1, keepdims=True))
        alpha = jnp.exp(m_old - m_new)
        p = jnp.exp(s - m_new)
        l_ref[...] = alpha * l_ref[...] + jnp.sum(p, axis=1, keepdims=True)
        m_ref[...] = m_new
        acc_ref[...] = alpha * acc_ref[...] + _dot_nt(p.astype(BF16), vals_t)

    @pl.when(j == 0)
    def _():
        qm8 = qm_ref[0].astype(F32)
        qn = jnp.concatenate([qm8[:, (2 * p) * SLAB:(2 * p) * SLAB + LANE]
                              + qm8[:, (2 * p + 1) * SLAB:(2 * p + 1) * SLAB + LANE] for p in range(4)], axis=1)
        qn = qn * gkn_ref[...]
        head512 = lax.broadcasted_iota(jnp.int32, (1, MLA_HEADS * NOPE), 1) // NOPE
        qn_bd = jnp.concatenate([jnp.where(head512 == h, qn, 0.0) for h in range(MLA_HEADS)], axis=0)
        lhs_s[0:MLA_HEADS * NOPE, :] = wkt_ref[...]
        lhs_s[MLA_HEADS * NOPE:, :] = _dot(qn_bd.astype(BF16), wkt_ref[...]).astype(BF16)
        qr_s[...] = jnp.concatenate([qm8[:, h * SLAB + LANE:h * SLAB + LANE + ROPE]
                                     for h in range(MLA_HEADS)], axis=0).astype(BF16)
        fq8 = fq_ref[0].astype(F32)
        zero = jnp.zeros((DEC_T, LANE), F32)
        blocks = []
        for h in range(FOX_HEADS):
            slab = fq8[:, h * SLAB:h * SLAB + LANE]
            blocks.append(jnp.concatenate([slab, zero] if (h // 2) // 2 == 0 else [zero, slab], axis=1))
        qbd_s[...] = jnp.concatenate(blocks, axis=0).astype(BF16)
        carry_s[...] = jnp.zeros_like(carry_s)
        m1[...] = jnp.full(m1.shape, NEG, F32)
        m2[...] = jnp.full(m2.shape, NEG, F32)
        l1[...] = jnp.zeros_like(l1)
        l2[...] = jnp.zeros_like(l2)
        ctx[...] = jnp.zeros_like(ctx)
        acc2[...] = jnp.zeros_like(acc2)

        tok = lax.broadcasted_iota(jnp.int32, (nrow, LANE), 0) % DEC_T
        valid = lax.broadcasted_iota(jnp.int32, (nrow, LANE), 1) <= tok
        lat_t = locm_ref[0, 0:KV_LORA, :].astype(BF16)
        s1 = mla_scores(lat_t, locm_ref[0, KV_LORA:MLA_ROW, :].astype(BF16))
        cum_t = _dot3(locf_ref[0], linc_ref[...])
        s2 = fox_scores(lock_ref[0].astype(BF16), -cum_t)
        s1_p, s2_p, lat_p, v_p = pend_b
        s1_p[...] = jnp.full(s1_p.shape, NEG, F32)
        s2_p[...] = jnp.full(s2_p.shape, NEG, F32)
        lat_p[...] = jnp.zeros_like(lat_p)
        v_p[...] = jnp.zeros_like(v_p)
        s1_p[:, 0:LANE] = jnp.where(valid, s1, NEG)
        s2_p[:, 0:LANE] = jnp.where(valid, s2, NEG)
        lat_p[:, 0:LANE] = lat_t
        v_p[:, 0:LANE] = locv_ref[0].astype(BF16)

    def score_phase(sl, queue):
        s1_p, s2_p, lat_p, v_p = queue
        lat_t = jnp.concatenate([mla_buf[sl, i, 0:KV_LORA, :].astype(BF16) for i in range(PPS)], axis=1)
        rope_t = jnp.concatenate([mla_buf[sl, i, KV_LORA:MLA_ROW, :].astype(BF16) for i in range(PPS)], axis=1)
        s1_p[...] = mla_scores(lat_t, rope_t)
        lat_p[...] = lat_t
        lfs = jnp.concatenate([lf_buf[sl, i] for i in range(PPS)], axis=0)
        res = _dot3(lfs, u_ref[...])
        carry = carry_s[...]
        biases = [None] * PPS
        for i in reversed(range(PPS)):
            biases[i] = res[i * 8:(i + 1) * 8, 0:LANE] + carry
            carry = carry + res[i * 8:(i + 1) * 8, LANE:2 * LANE]
        carry_s[...] = carry
        k_t = jnp.concatenate([fk_buf[sl, i].astype(BF16) for i in range(PPS)], axis=1)
        s2_p[...] = fox_scores(k_t, jnp.concatenate(biases, axis=1))
        v_p[...] = jnp.concatenate([fv_buf[sl, i].astype(BF16) for i in range(PPS)], axis=1)

    def value_phase(queue):
        s1_p, s2_p, lat_p, v_p = queue
        update(s1_p[...], m1, l1, ctx, lat_p[...])
        update(s2_p[...], m2, l2, acc2, v_p[...])

    @pl.when(j % 2 == 0)
    def _():
        score_phase(0, pend_a)
        value_phase(pend_b)

    @pl.when(j % 2 == 1)
    def _():
        score_phase(1, pend_b)
        value_phase(pend_a)

    @pl.when(j == nj - 1)
    def _():
        value_phase(pend_b)
        full = _dot((ctx[...] * (1.0 / l1[...])).astype(BF16), wv_ref[...])
        head512 = lax.broadcasted_iota(jnp.int32, (1, MLA_HEADS * MLA_V), 1) // MLA_V
        o_mla = jnp.where(head512 == 0, full[0:DEC_T], 0.0)
        for h in range(1, MLA_HEADS):
            o_mla = o_mla + jnp.where(head512 == h, full[h * DEC_T:(h + 1) * DEC_T], 0.0)
        a2 = acc2[...] * (1.0 / l2[...])
        head256 = lax.broadcasted_iota(jnp.int32, (1, FOX_KV * FOX_DIM), 1) // FOX_DIM
        outs = []
        for grp in range(FOX_HEADS // FOX_KV):
            o = None
            for kvh in range(FOX_KV):
                r0 = (kvh * 2 + grp) * DEC_T
                term = jnp.where(head256 == kvh, a2[r0:r0 + DEC_T], 0.0)
                o = term if o is None else o + term
            outs.append(o)
        o_ref[0] = jnp.concatenate([o_mla] + outs, axis=1)


def _decode(page_table_flat, qm, fq, cum, locm, lock, locv, locf, cm, ck, cv, cl, w, n_seq, n_pages):
    nj = n_pages // PPS
    assert nj % 2 == 0, "buffer slots alternate with the step parity within a sequence"
    keys = PPS * LANE

    seq = lambda rows, width: pl.BlockSpec((1, rows, width), lambda b, j, pt: (b, 0, 0))
    cspec = lambda a: pl.BlockSpec(a.shape, lambda b, j, pt: (0,) * a.ndim)
    consts = [w['w_kt'], w['w_v'], w['g_kn512'], w['u_mat'], w['l_inc']]
    nrow = MLA_HEADS * DEC_T
    grid_spec = pltpu.PrefetchScalarGridSpec(
        num_scalar_prefetch=1,
        grid=(n_seq, nj),
        in_specs=[seq(DEC_T, MLA_HEADS * SLAB), seq(DEC_T, FOX_HEADS * SLAB), seq(DEC_T, LANE),
                  seq(MLA_ROW, LANE), seq(FOX_KV * FOX_DIM, LANE), seq(FOX_KV * FOX_DIM, LANE), seq(FOX_HEADS, LANE)]
        + [cspec(c) for c in consts] + [pl.BlockSpec(memory_space=pl.ANY)] * 4,
        out_specs=pl.BlockSpec((1, DEC_T, D_MODEL), lambda b, j, pt: (b, 0, 0)),
        scratch_shapes=[pltpu.VMEM((2, PPS, MLA_ROW, LANE), F32), pltpu.VMEM((2, PPS, FOX_KV * FOX_DIM, LANE), F32),
                        pltpu.VMEM((2, PPS, FOX_KV * FOX_DIM, LANE), F32), pltpu.VMEM((2, PPS, FOX_HEADS, LANE), F32),
                        pltpu.SemaphoreType.DMA((2, 4)),
                        pltpu.VMEM((MLA_HEADS * NOPE + nrow, KV_LORA), BF16), pltpu.VMEM((nrow, ROPE), BF16),
                        pltpu.VMEM((nrow, FOX_KV * FOX_DIM), BF16), pltpu.VMEM((FOX_HEADS, LANE), F32),
                        pltpu.VMEM((nrow, 1), F32), pltpu.VMEM((nrow, 1), F32), pltpu.VMEM((nrow, KV_LORA), F32),
                        pltpu.VMEM((nrow, 1), F32), pltpu.VMEM((nrow, 1), F32),
                        pltpu.VMEM((nrow, FOX_KV * FOX_DIM), F32)]
        + [pltpu.VMEM((nrow, keys), F32), pltpu.VMEM((nrow, keys), F32), pltpu.VMEM((KV_LORA, keys), BF16),
           pltpu.VMEM((FOX_KV * FOX_DIM, keys), BF16)] * 2)
    return pl.pallas_call(
        functools.partial(_decode_kernel, nj=nj, n_pages=n_pages, n_seq=n_seq),
        grid_spec=grid_spec,
        out_shape=jax.ShapeDtypeStruct((n_seq, DEC_T, D_MODEL), F32),
        compiler_params=pltpu.CompilerParams(dimension_semantics=("arbitrary", "arbitrary"),
                                             vmem_limit_bytes=VMEM_LIMIT),
        name="decode",
    )(page_table_flat, qm, fq, cum, locm, lock, locv, locf, *consts, cm, ck, cv, cl)


def _place(vec, offset, width=LANE):
    return jnp.zeros((width,), F32).at[offset:offset + vec.shape[0]].set(vec)


def _rot_half(v, axis=-1):
    a, b = jnp.split(v, 2, axis=axis)
    return jnp.concatenate([-b, a], axis=axis)


def _prep_weights(g_mix, w_in, g_q_lat, w_q_up, g_kv_lat, g_k_rope, g_q_nope, g_q_rope, g_k_nope, w_kv_up, g_fox_q,
                  g_fox_k, b_forget, w_o, g_cross, g_mem, w_mq, w_mk, w_mv, g_mq, g_mk, w_mo, g_ffn, w_up, w_down):
    w = {}
    win = w_in[0]
    seg = [win[:, IN_SPLITS[k]:IN_SPLITS[k + 1]] for k in range(7)]
    c_q, c_kv, k_r, f_q, f_k, f_v, f_l = seg
    zc = lambda n: jnp.zeros((D_MODEL, n), F32)
    cols = [c_q, c_kv, k_r, zc(LANE - ROPE), _rot_half(k_r), zc(LANE - ROPE)]
    for h in range(FOX_HEADS):
        off = ((h // 2) % 2) * FOX_DIM
        cols += [zc(off), f_q[:, h * FOX_DIM:(h + 1) * FOX_DIM], zc(LANE - FOX_DIM - off)]
    cols += [f_k, f_v, f_l, zc(LANE - FOX_HEADS)]
    w['w_in'] = jnp.concatenate([c for c in cols if c.shape[1]], axis=1).astype(BF16)

    wq = w_q_up[0]
    zq = lambda n: jnp.zeros((Q_LORA, n), F32)
    qcols, rcols = [], []
    for h in range(MLA_HEADS):
        base = h * (NOPE + ROPE)
        off = (h % 2) * NOPE
        rope_w = wq[:, base + NOPE:base + NOPE + ROPE]
        qcols += [zq(off), wq[:, base:base + NOPE], zq(LANE - NOPE - off), rope_w, zq(LANE - ROPE)]
        rcols += [_rot_half(rope_w), zq(LANE - ROPE)]
    w['w_q'] = jnp.concatenate([c for c in qcols + rcols if c.shape[1]], axis=1).astype(BF16)

    wkv = w_kv_up[0].reshape(KV_LORA, MLA_HEADS, NOPE + MLA_V)
    w_k = wkv[:, :, :NOPE].reshape(KV_LORA, MLA_HEADS * NOPE)
    w_v = wkv[:, :, NOPE:].reshape(KV_LORA, MLA_HEADS * MLA_V)
    w['w_kv'] = jnp.concatenate([w_k, w_v], axis=1).astype(BF16)
    w['w_kt'] = w_k.T.astype(BF16)
    w['w_v'] = w_v.astype(BF16)
    w['g_kn512'] = jnp.tile(g_k_nope[0], MLA_HEADS)[None, :]

    gqr, gkr = g_q_rope[0] * MLA_SCALE, g_k_rope[0]
    swap = lambda v: jnp.concatenate([v[HALF:], v[:HALF]])
    k_aug_base = jnp.zeros((LANE,), F32).at[0:3].set(1.0)
    rows = [_place(g_q_nope[0] * MLA_SCALE, 0), _place(g_q_nope[0] * MLA_SCALE, NOPE), _place(gqr, 0),
            _place(swap(gqr), 0), _place(gkr, 0), _place(swap(gkr), 0), jnp.tile(g_k_nope[0], 2),
            _place(g_fox_q[0] * FOX_SCALE, 0), _place(g_fox_q[0] * FOX_SCALE, FOX_DIM), jnp.tile(g_fox_k[0], 2),
            _place(b_forget[0], 0), k_aug_base]
    rows += [jnp.zeros((LANE,), F32)] * (16 - len(rows))
    w['g128'] = jnp.stack(rows)

    place = np.zeros((3, LANE, LANE + FOX_HEADS * LANE), np.float32)
    base_q = np.zeros((1, FOX_HEADS * LANE), np.float32)
    for piece in range(3):
        for h in range(FOX_HEADS):
            place[piece, h, 3 + 3 * h + piece] = -1.0
            place[piece, h, LANE + h * LANE + piece] = 1.0
            base_q[0, h * LANE + 3 + 3 * h + piece] = 1.0
    w['place'] = jnp.asarray(place, BF16)
    w['base_q'] = jnp.asarray(base_q)

    w['g_mix'], w['g_q_lat'], w['g_kv_lat'] = g_mix, g_q_lat, g_kv_lat
    w['g_cross'], w['g_mem'], w['g_ffn'] = g_cross, g_mem, g_ffn
    w['g_mq'] = jnp.tile(g_mq[0] * MEM_SCALE, MEM_HEADS)[None, :]
    w['g_mk'] = jnp.tile(g_mk[0], MEM_HEADS)[None, :]
    w['w_mkv'] = jnp.concatenate([w_mk[0], w_mv[0]], axis=1).astype(BF16)
    w['w_mq'] = w_mq[0].astype(BF16)
    w['w_mo'] = w_mo[0].astype(BF16)
    w['w_up'] = w_up[0].astype(BF16)
    w['w_down'] = w_down[0].astype(BF16)

    wo = w_o[0]
    mla_w = MLA_HEADS * MLA_V
    fox_rows = wo[mla_w:].reshape(FOX_KV // 2, 2, 2, FOX_DIM, D_MODEL)
    prompt_fox = jnp.transpose(fox_rows, (0, 2, 1, 3, 4)).reshape(FOX_HEADS * FOX_DIM, D_MODEL)
    w['w_o'] = jnp.concatenate([wo[:mla_w], prompt_fox], axis=0).astype(BF16)
    fox_rows = wo[mla_w:].reshape(FOX_KV, 2, FOX_DIM, D_MODEL)
    sample_fox = jnp.transpose(fox_rows, (1, 0, 2, 3)).reshape(FOX_HEADS * FOX_DIM, D_MODEL)
    w['w_o_sample'] = jnp.concatenate([wo[:mla_w], sample_fox], axis=0).astype(BF16)

    idx = np.arange(LANE)
    u_strict = (idx[:, None] > idx[None, :]).astype(np.float32)
    w['u_mat'] = jnp.asarray(np.concatenate([u_strict, np.ones((LANE, LANE), np.float32)], axis=1), BF16)
    w['l_inc'] = jnp.asarray((idx[:, None] <= idx[None, :]).astype(np.float32), BF16)
    return w


def _rope_tables(pos):
    inv_freq = ROPE_BASE ** (-jnp.arange(HALF, dtype=F32) / HALF)
    ang = pos.astype(F32)[:, None] * inv_freq[None, :]
    pad = jnp.zeros((pos.shape[0], LANE - ROPE), F32)
    cos = jnp.concatenate([jnp.cos(ang), jnp.cos(ang), pad], axis=1)
    sin = jnp.concatenate([jnp.sin(ang), jnp.sin(ang), pad], axis=1)
    return cos, sin


def _tri(tm, group):
    idx = np.arange(tm)
    m = (idx[None, :] <= idx[:, None]) & (idx[None, :] // group == idx[:, None] // group)
    return jnp.asarray(m.astype(np.float32), BF16)


def kernel(x_prompt, x_sample, cache_mla, cache_fox_k, cache_fox_v, cache_fox_logf, cache_mem_k, cache_mem_v,
           page_table, mem_prompt, g_mix, w_in, g_q_lat, w_q_up, g_kv_lat, g_k_rope, g_q_nope, g_q_rope, g_k_nope,
           w_kv_up, g_fox_q, g_fox_k, b_forget, w_o, g_cross, g_mem, w_mq, w_mk, w_mv, g_mq, g_mk, w_mo, g_ffn, w_up,
           w_down):
    assert cache_mla.shape[0] == 1, "single-layer kernel"
    batch, s_len, _ = x_prompt.shape
    n_seq, t_new, _ = x_sample.shape
    n_pool, page = cache_mla.shape[1], cache_mla.shape[2]
    n_pages = page_table.shape[1]
    mem_len = mem_prompt.shape[1]
    assert page == LANE and t_new <= DEC_T and n_pages % PPS == 0
    assert s_len % PROJ_TM == 0 and s_len % ATTN_T == 0 and s_len % FIN_TM == 0
    w = _prep_weights(g_mix, w_in, g_q_lat, w_q_up, g_kv_lat, g_k_rope, g_q_nope, g_q_rope, g_k_nope, w_kv_up,
                      g_fox_q, g_fox_k, b_forget, w_o, g_cross, g_mem, w_mq, w_mk, w_mv, g_mq, g_mk, w_mo, g_ffn,
                      w_up, w_down)

    n = batch * s_len
    xp = x_prompt.reshape(n, D_MODEL)
    cos_p, sin_p = _rope_tables(jnp.arange(s_len))
    (p_row_t, qm, kn, vm_t, kr, fq, fk_t, fkb, fv_t, fvb_t, p_lf_t, _, kaug) = _proj(
        xp, cos_p, sin_p, _tri(PROJ_TM, PROJ_TM), w, s_len // PROJ_TM)
    o_mla = _attention(qm, kn, kr, vm_t, batch, s_len, 2, ((0, 1),), "attn_mla")
    o_fox = _attention(fq, fkb, kaug, fvb_t, batch, s_len, 4, ((0, 2), (1, 3)), "attn_fox")
    mk_t, mv_t = _memkv(mem_prompt.reshape(batch * mem_len, D_MODEL), batch, mem_len, w)
    y_prompt = _finish(xp, jnp.concatenate([o_mla, o_fox], axis=1), mk_t, mv_t, w['w_o'], w, 1, FIN_TM,
                       s_len // FIN_TM, "finish_prompt")

    ns = n_seq * DEC_T
    xs = jnp.pad(x_sample, ((0, 0), (0, DEC_T - t_new), (0, 0))).reshape(ns, D_MODEL)
    pos_s = n_pages * page + (jnp.arange(PROJ_TM) % DEC_T)
    cos_s, sin_s = _rope_tables(pos_s)
    (s_row_t, qm_s, _, _, _, fq_s, fk_s_t, _, fv_s_t, _, s_lf_t, cum_s, _) = _proj(
        xs, cos_s, sin_s, _tri(PROJ_TM, DEC_T), w, 1)

    def new_tokens(a_t):
        feat = a_t.shape[0] // (ns // PROJ_TM)
        a = jnp.transpose(a_t.reshape(ns // PROJ_TM, feat, PROJ_TM), (1, 0, 2))
        return a.reshape(feat, n_seq, DEC_T)[:, :, :t_new]

    def new_page(a_t):
        return jnp.pad(jnp.transpose(new_tokens(a_t), (1, 0, 2)), ((0, 0), (0, 0), (0, LANE - t_new)))

    cm = jnp.transpose(cache_mla[0], (0, 2, 1))
    ck = jnp.transpose(cache_fox_k[0], (0, 2, 3, 1)).reshape(n_pool, FOX_KV * FOX_DIM, page)
    cv = jnp.transpose(cache_fox_v[0], (0, 2, 3, 1)).reshape(n_pool, FOX_KV * FOX_DIM, page)
    cl = jnp.transpose(cache_fox_logf[0], (0, 2, 1))
    o_s = _decode(page_table.reshape(-1), qm_s.reshape(n_seq, DEC_T, -1), fq_s.reshape(n_seq, DEC_T, -1),
                  cum_s.reshape(n_seq, DEC_T, LANE), new_page(s_row_t), new_page(fk_s_t), new_page(fv_s_t),
                  new_page(s_lf_t), cm, ck, cv, cl, w, n_seq, n_pages)
    cmk = jnp.transpose(cache_mem_k[0], (0, 2, 3, 1)).reshape(n_seq, MEM_W, mem_len)
    cmv = jnp.transpose(cache_mem_v[0], (0, 2, 3, 1)).reshape(n_seq, MEM_W, mem_len)
    y_s = _finish(xs, o_s.reshape(ns, D_MODEL), cmk, cmv, w['w_o_sample'], w, 16, DEC_T, 1, "finish_sample")

    def tok(a_t, *shape):
        return jnp.transpose(new_tokens(a_t), (1, 2, 0)).reshape(1, n_seq, t_new, *shape)

    def seq_out(a_t, *shape):
        return jnp.transpose(a_t.reshape(batch, -1, a_t.shape[1]), (0, 2, 1)).reshape(1, batch, a_t.shape[1], *shape)

    y_sample = y_s.reshape(n_seq, DEC_T, D_MODEL)[:, :t_new]
    return (y_prompt.reshape(batch, s_len, D_MODEL), y_sample,
            seq_out(p_row_t, MLA_ROW), seq_out(fk_t, FOX_KV, FOX_DIM), seq_out(fv_t, FOX_KV, FOX_DIM),
            seq_out(p_lf_t, FOX_HEADS), seq_out(mk_t.reshape(batch * MEM_W, mem_len), MEM_HEADS, MEM_DIM),
            seq_out(mv_t.reshape(batch * MEM_W, mem_len), MEM_HEADS, MEM_DIM),
            tok(s_row_t, MLA_ROW), tok(fk_s_t, FOX_KV, FOX_DIM), tok(fv_s_t, FOX_KV, FOX_DIM),
            tok(s_lf_t, FOX_HEADS))
```

```python
import functools

import numpy as np
import jax
import jax.numpy as jnp
from jax import lax
from jax.experimental import pallas as pl
from jax.experimental.pallas import tpu as pltpu

F32 = jnp.float32
BF16 = jnp.bfloat16

D_MODEL = 1024
Q_LORA = 384
KV_LORA = 256
ROPE = 32
HALF = ROPE // 2
NOPE = 64
MLA_V = 64
MLA_HEADS = 8
MLA_ROW = KV_LORA + ROPE
FOX_HEADS = 8
FOX_KV = 4
FOX_DIM = 64
MEM_HEADS = 4
MEM_DIM = 64
MEM_W = MEM_HEADS * MEM_DIM
D_FF = 4096
IN_SPLITS = (0, 384, 640, 672, 1184, 1440, 1696, 1704)
ROPE_BASE = 10000.0
EPS = 1e-6
NEG = -1e30
MLA_SCALE = (NOPE + ROPE) ** -0.5
FOX_SCALE = FOX_DIM ** -0.5
MEM_SCALE = MEM_DIM ** -0.5

LANE = 128
SLAB = 2 * LANE
VMEM_LIMIT = 56 * 1024 * 1024
PROJ_TM = 256
ATTN_T = 512
FIN_TM = 512
DEC_T = 8
PPS = 32
FFN_CHUNK = 1024
ONES_ROWS = 16

C_CQ, C_CKV, C_KR, C_KROT, C_FQ, C_FK, C_FV, C_FL, W_IN_EXP = 0, 384, 640, 768, 896, 1920, 2176, 2432, 2560
G_QN_E, G_QN_O, G_QR, G_QR_ROT, G_KR, G_KR_ROT, G_KN, G_FQ_E, G_FQ_O, G_FK, G_BF, G_KAUG = range(12)


def _const_spec(shape):
    n = len(shape)
    return pl.BlockSpec(shape, lambda *_: (0,) * n, pipeline_mode=pl.Buffered(1))


def _rms(x, width):
    return x * lax.rsqrt(jnp.sum(x * x, axis=-1, keepdims=True) * (1.0 / width) + EPS)


def _group64_rms(x):
    out = []
    lane = lax.broadcasted_iota(jnp.int32, (1, LANE), 1)
    lo = lane < 64
    for j in range(x.shape[1] // LANE):
        s = x[:, j * LANE:(j + 1) * LANE]
        s2 = s * s
        r_lo = lax.rsqrt(jnp.sum(jnp.where(lo, s2, 0.0), axis=-1, keepdims=True) * (1.0 / 64) + EPS)
        r_hi = lax.rsqrt(jnp.sum(jnp.where(lo, 0.0, s2), axis=-1, keepdims=True) * (1.0 / 64) + EPS)
        out.append(s * jnp.where(lo, r_lo, r_hi))
    return out[0] if len(out) == 1 else jnp.concatenate(out, axis=1)


def _split3(x):
    hi = x.astype(BF16)
    r1 = x - hi.astype(F32)
    mid = r1.astype(BF16)
    lo = (r1 - mid.astype(F32)).astype(BF16)
    return hi, mid, lo


def _dot(a, b):
    return jnp.dot(a, b, preferred_element_type=F32)


def _dot_nt(a, b):
    return lax.dot_general(a, b, (((1,), (1,)), ((), ())), preferred_element_type=F32)


def _dot3(x, w):
    hi, mid, lo = _split3(x)
    return _dot(hi, w) + _dot(mid, w) + _dot(lo, w)


def _proj_kernel(x_ref, cos_ref, sin_ref, tri_ref, gmix_ref, win_ref, gql_ref, wq_ref, gkv_ref, wkv_ref,
                 g128_ref, place_ref, baseq_ref,
                 rowt_ref, qm_ref, kn_ref, vmt_ref, kr_ref, fq_ref, fkt_ref, fkb_ref, fvt_ref, fvbt_ref,
                 lft_ref, cum_ref, kaug_ref, carry_ref, *, tiles_per_seq):
    i = pl.program_id(0)
    cos = cos_ref[...]
    sin = sin_ref[...]

    def g(r):
        return g128_ref[r:r + 1, :]

    xn = (_rms(x_ref[...], D_MODEL) * gmix_ref[...]).astype(BF16)
    a = _dot(xn, win_ref[...])

    cq = (_rms(a[:, C_CQ:C_CQ + Q_LORA], Q_LORA) * gql_ref[...]).astype(BF16)
    q = _dot(cq, wq_ref[...])
    for h in range(MLA_HEADS):
        nope = q[:, h * SLAB:h * SLAB + LANE]
        rope = q[:, h * SLAB + LANE:(h + 1) * SLAB]
        rot = q[:, MLA_HEADS * SLAB + h * LANE:MLA_HEADS * SLAB + (h + 1) * LANE]
        rn = lax.rsqrt(jnp.sum(nope * nope, axis=-1, keepdims=True) * (1.0 / NOPE) + EPS)
        rr = lax.rsqrt(jnp.sum(rope * rope, axis=-1, keepdims=True) * (1.0 / ROPE) + EPS)
        qm_ref[:, h * SLAB:h * SLAB + LANE] = (nope * rn * g(G_QN_O if h % 2 else G_QN_E)).astype(BF16)
        qm_ref[:, h * SLAB + LANE:(h + 1) * SLAB] = (
            rr * (rope * g(G_QR) * cos + rot * g(G_QR_ROT) * sin)).astype(BF16)

    lat = _rms(a[:, C_CKV:C_CKV + KV_LORA], KV_LORA) * gkv_ref[...]
    kr = a[:, C_KR:C_KR + LANE]
    krot = a[:, C_KROT:C_KROT + LANE]
    rk = lax.rsqrt(jnp.sum(kr * kr, axis=-1, keepdims=True) * (1.0 / ROPE) + EPS)
    k_rope = rk * (kr * g(G_KR) * cos + krot * g(G_KR_ROT) * sin)
    rowt_ref[0:KV_LORA, :] = lat.T
    rowt_ref[KV_LORA:MLA_ROW, :] = k_rope.T[0:ROPE]
    kr_ref[...] = k_rope.astype(BF16)

    kv = _dot(lat.astype(BF16), wkv_ref[...])
    gkn = g(G_KN)
    kn = _group64_rms(kv[:, 0:MLA_HEADS * NOPE])
    kn_ref[...] = (kn * jnp.concatenate([gkn] * 4, axis=1)).astype(BF16)
    vmt_ref[...] = kv[:, MLA_HEADS * NOPE:].T.astype(BF16)

    for h in range(FOX_HEADS):
        s = a[:, C_FQ + h * LANE:C_FQ + (h + 1) * LANE]
        r = lax.rsqrt(jnp.sum(s * s, axis=-1, keepdims=True) * (1.0 / FOX_DIM) + EPS)
        fq_ref[:, h * SLAB:h * SLAB + LANE] = (s * r * g(G_FQ_O if (h // 2) % 2 else G_FQ_E)).astype(BF16)
    fk = _group64_rms(a[:, C_FK:C_FK + 2 * LANE]) * jnp.concatenate([g(G_FK)] * 2, axis=1)
    fkt_ref[...] = fk.T
    fkb_ref[...] = fk.astype(BF16)
    fvt = a[:, C_FV:C_FV + 2 * LANE].T
    fvt_ref[...] = fvt
    fvbt_ref[...] = fvt.astype(BF16)

    z = a[:, C_FL:C_FL + LANE] + g(G_BF)
    lane = lax.broadcasted_iota(jnp.int32, (1, LANE), 1)
    logf = jnp.where(lane < FOX_HEADS, jnp.minimum(z, 0.0) - jnp.log1p(jnp.exp(-jnp.abs(z))), 0.0)
    lft_ref[...] = logf.T[0:FOX_HEADS]

    @pl.when(i % tiles_per_seq == 0)
    def _():
        carry_ref[...] = jnp.zeros_like(carry_ref)

    l_hi, l_mid, l_lo = _split3(logf)
    tri = tri_ref[...]
    parts = _dot(tri, jnp.concatenate([l_hi, l_mid, l_lo], axis=1))
    cum = parts[:, 0:LANE] + parts[:, LANE:2 * LANE] + parts[:, 2 * LANE:] + carry_ref[0:1, :]
    cum_ref[...] = cum
    tm = cum.shape[0]
    carry_ref[...] = jnp.broadcast_to(cum[tm - 1:tm, :], carry_ref.shape)

    c_hi, c_mid, c_lo = _split3(cum)
    aug = _dot(jnp.concatenate([c_hi, c_mid, c_lo], axis=1), place_ref[...])
    kaug_ref[...] = (aug[:, 0:LANE] + g(G_KAUG)).astype(BF16)
    for h in range(FOX_HEADS):
        fq_ref[:, h * SLAB + LANE:(h + 1) * SLAB] = (
            aug[:, (h + 1) * LANE:(h + 2) * LANE] + baseq_ref[:, h * LANE:(h + 1) * LANE]).astype(BF16)


def _proj(x2, cos_t, sin_t, tri, w, tiles_per_seq):
    n = x2.shape[0]
    tm = PROJ_TM
    n_tab = cos_t.shape[0] // tm
    n_seq = n // (tiles_per_seq * tm)
    s_len = tiles_per_seq * tm
    row = lambda width: pl.BlockSpec((tm, width), lambda i: (i, 0))
    tab = pl.BlockSpec((tm, LANE), lambda i: (i % n_tab, 0))
    outs = [('t', MLA_ROW, F32), ('r', MLA_HEADS * SLAB, BF16), ('r', MLA_HEADS * NOPE, BF16),
            ('t', MLA_HEADS * MLA_V, BF16), ('r', LANE, BF16), ('r', FOX_HEADS * SLAB, BF16),
            ('t', FOX_KV * FOX_DIM, F32), ('r', FOX_KV * FOX_DIM, BF16), ('t', FOX_KV * FOX_DIM, F32),
            ('t', FOX_KV * FOX_DIM, BF16), ('t', FOX_HEADS, F32), ('r', LANE, F32), ('r', LANE, BF16)]
    out_specs, out_shape = [], []
    for kind, width, dtype in outs:
        if kind == 'r':
            out_specs.append(row(width))
            out_shape.append(jax.ShapeDtypeStruct((n, width), dtype))
        else:
            out_specs.append(pl.BlockSpec((width, tm), lambda i: (i // tiles_per_seq, i % tiles_per_seq)))
            out_shape.append(jax.ShapeDtypeStruct((n_seq * width, s_len), dtype))
    consts = [w['g_mix'], w['w_in'], w['g_q_lat'], w['w_q'], w['g_kv_lat'], w['w_kv'], w['g128'], w['place'],
              w['base_q']]
    return pl.pallas_call(
        functools.partial(_proj_kernel, tiles_per_seq=tiles_per_seq),
        grid=(n // tm,),
        in_specs=[row(D_MODEL), tab, tab, _const_spec(tri.shape)] + [_const_spec(c.shape) for c in consts],
        out_specs=out_specs,
        out_shape=out_shape,
        scratch_shapes=[pltpu.VMEM((8, LANE), F32)],
        compiler_params=pltpu.CompilerParams(dimension_semantics=("arbitrary",), vmem_limit_bytes=VMEM_LIMIT),
        name="proj",
    )(x2, cos_t, sin_t, tri, *consts)


def _attn_kernel(q_ref, km_ref, ke_ref, vt_ref, o_ref, qt_s, qt_s2, s_a, s_b, m_s, acc_s, *, hb, t, s_len, combos):
    cols = hb * t
    key_i = lax.broadcasted_iota(jnp.int32, (t, cols), 0)
    causal = key_i <= lax.broadcasted_iota(jnp.int32, (t, cols), 1) % t
    lo = lax.broadcasted_iota(jnp.int32, (LANE, 1), 0) < 64
    ones = jnp.ones((ONES_ROWS, t), BF16)

    qts = (qt_s, qt_s2)
    bufs = (s_a, s_b)

    def prepare_q(qb):
        r0 = qb * t
        qts[qb % 2][...] = jnp.concatenate(
            [q_ref[r0:r0 + t, h * SLAB:(h + 1) * SLAB].astype(F32).T for h in range(hb)], axis=1).astype(BF16)

    def scores(qb, kb, buf):
        kc = jnp.concatenate([km_ref[kb * t:(kb + 1) * t, :], ke_ref[kb * t:(kb + 1) * t, :]], axis=1)
        buf[...] = _dot(kc, qts[qb % 2][...])

    def update(buf, kb, masked):
        s = buf[...]
        if masked:
            s = jnp.where(causal, s, NEG)
        m = m_s[...]
        m_new = jnp.maximum(m, jnp.max(s, axis=0, keepdims=True))
        alpha = jnp.exp(m - m_new)
        p = jnp.exp(s - m_new).astype(BF16)
        vt1 = jnp.concatenate([vt_ref[:, kb * t:(kb + 1) * t], ones], axis=0)
        m_s[...] = m_new
        acc_s[...] = alpha * acc_s[...] + _dot(vt1, p)

    def finish_q(qb):
        acc = acc_s[...]
        o = acc[0:LANE] * (1.0 / acc[LANE:LANE + 1])
        out = [jnp.where(lo, o[:, a * t:(a + 1) * t], o[:, b * t:(b + 1) * t]).T for a, b in combos]
        o_ref[qb * t:(qb + 1) * t, :] = jnp.concatenate(out, axis=1).astype(o_ref.dtype)

    blocks = [(qb, kb) for qb in range(s_len // t) for kb in range(qb + 1)]
    prepare_q(0)
    scores(0, 0, bufs[0])
    for n, (qb, kb) in enumerate(blocks):
        if n + 1 < len(blocks):
            nqb, nkb = blocks[n + 1]
            if nkb == 0:
                prepare_q(nqb)
            scores(nqb, nkb, bufs[(n + 1) % 2])
        if kb == 0:
            m_s[...] = jnp.full(m_s.shape, NEG, F32)
            acc_s[...] = jnp.zeros_like(acc_s)
        update(bufs[n % 2], kb, kb == qb)
        if kb == qb:
            finish_q(qb)


def _attention(q, km, ke, vt, batch, s_len, hb, combos, name):
    n_pairs = km.shape[1] // LANE
    ow = len(combos) * LANE
    cols = hb * ATTN_T
    return pl.pallas_call(
        functools.partial(_attn_kernel, hb=hb, t=ATTN_T, s_len=s_len, combos=combos),
        grid=(batch, n_pairs),
        in_specs=[pl.BlockSpec((s_len, hb * SLAB), lambda b, p: (b, p)),
                  pl.BlockSpec((s_len, LANE), lambda b, p: (b, p)),
                  pl.BlockSpec((s_len, LANE), lambda b, p: (b, 0)),
                  pl.BlockSpec((LANE, s_len), lambda b, p: (b * n_pairs + p, 0))],
        out_specs=pl.BlockSpec((s_len, ow), lambda b, p: (b, p)),
        out_shape=jax.ShapeDtypeStruct((batch * s_len, n_pairs * ow), BF16),
        scratch_shapes=[pltpu.VMEM((SLAB, cols), BF16), pltpu.VMEM((SLAB, cols), BF16), pltpu.VMEM((ATTN_T, cols), F32),
                        pltpu.VMEM((ATTN_T, cols), F32), pltpu.VMEM((1, cols), F32),
                        pltpu.VMEM((LANE + ONES_ROWS, cols), F32)],
        compiler_params=pltpu.CompilerParams(dimension_semantics=("arbitrary", "arbitrary"),
                                             vmem_limit_bytes=VMEM_LIMIT),
        name=name,
    )(q, km, ke, vt)


def _memkv_kernel(mem_ref, gmem_ref, w_ref, gmk_ref, kt_ref, vt_ref):
    mn = (_rms(mem_ref[...], D_MODEL) * gmem_ref[...]).astype(BF16)
    kv = _dot(mn, w_ref[...])
    k = _group64_rms(kv[:, 0:MEM_W]) * gmk_ref[...]
    kt_ref[0] = k.T
    vt_ref[0] = kv[:, MEM_W:].T


def _memkv(mem2, batch, mem_len, w):
    return pl.pallas_call(
        _memkv_kernel,
        grid=(batch,),
        in_specs=[pl.BlockSpec((mem_len, D_MODEL), lambda b: (b, 0)), _const_spec(w['g_mem'].shape),
                  _const_spec(w['w_mkv'].shape), _const_spec(w['g_mk'].shape)],
        out_specs=[pl.BlockSpec((1, MEM_W, mem_len), lambda b: (b, 0, 0))] * 2,
        out_shape=[jax.ShapeDtypeStruct((batch, MEM_W, mem_len), F32)] * 2,
        compiler_params=pltpu.CompilerParams(dimension_semantics=("arbitrary",), vmem_limit_bytes=VMEM_LIMIT),
        name="memkv",
    )(mem2, w['g_mem'], w['w_mkv'], w['g_mk'])


def _finish_kernel(x_ref, om_ref, mk_ref, mv_ref, wo_ref, gcross_ref, wmq_ref, gmq_ref, wmo_ref, gffn_ref,
                   wup_ref, wdown_ref, y_ref, q_scr, o_scr, *, spt, ts):
    h = x_ref[...] + _dot(om_ref[...].astype(BF16), wo_ref[...])
    hn = (_rms(h, D_MODEL) * gcross_ref[...]).astype(BF16)
    q_scr[...] = _group64_rms(_dot(hn, wmq_ref[...])) * gmq_ref[...]
    head_of_lane = lax.broadcasted_iota(jnp.int32, (1, MEM_W), 1) // MEM_DIM

    def one_seq(s, _):
        r0 = pl.multiple_of(s * ts, ts)
        q = q_scr[pl.ds(r0, ts), :]
        qs = jnp.concatenate([jnp.where(head_of_lane == hd, q, 0.0) for hd in range(MEM_HEADS)], axis=0)
        sc = _dot(qs.astype(BF16), mk_ref[s].astype(BF16))
        p = jnp.exp(sc - jnp.max(sc, axis=1, keepdims=True))
        l = jnp.sum(p, axis=1, keepdims=True)
        pv = _dot_nt(p.astype(BF16), mv_ref[s].astype(BF16)) * (1.0 / l)
        o = jnp.where(head_of_lane == 0, pv[0:ts], 0.0)
        for hd in range(1, MEM_HEADS):
            o = o + jnp.where(head_of_lane == hd, pv[hd * ts:(hd + 1) * ts], 0.0)
        o_scr[pl.ds(r0, ts), :] = o
        return 0

    if spt == 1:
        one_seq(0, 0)
    else:
        lax.fori_loop(0, spt, one_seq, 0)

    h = h + _dot(o_scr[...].astype(BF16), wmo_ref[...])
    hn = (_rms(h, D_MODEL) * gffn_ref[...]).astype(BF16)
    y = h
    for c in range(D_FF // FFN_CHUNK):
        u = jnp.maximum(_dot(hn, wup_ref[:, c * FFN_CHUNK:(c + 1) * FFN_CHUNK]), 0.0)
        y = y + _dot((u * u).astype(BF16), wdown_ref[c * FFN_CHUNK:(c + 1) * FFN_CHUNK, :])
    y_ref[...] = y


def _finish(x2, om2, mk, mv, w_o, w, spt, ts, tiles_per_seq, name):
    assert spt == 1 or tiles_per_seq == 1
    n = x2.shape[0]
    tm = spt * ts
    mem_len = mk.shape[2]
    mem_spec = pl.BlockSpec((spt, MEM_W, mem_len), lambda i: (i // tiles_per_seq, 0, 0))
    consts = [w_o, w['g_cross'], w['w_mq'], w['g_mq'], w['w_mo'], w['g_ffn'], w['w_up'], w['w_down']]
    return pl.pallas_call(
        functools.partial(_finish_kernel, spt=spt, ts=ts),
        grid=(n // tm,),
        in_specs=[pl.BlockSpec((tm, D_MODEL), lambda i: (i, 0)), pl.BlockSpec((tm, D_MODEL), lambda i: (i, 0)),
                  mem_spec, mem_spec] + [_const_spec(c.shape) for c in consts],
        out_specs=pl.BlockSpec((tm, D_MODEL), lambda i: (i, 0)),
        out_shape=jax.ShapeDtypeStruct((n, D_MODEL), F32),
        scratch_shapes=[pltpu.VMEM((tm, MEM_W), F32), pltpu.VMEM((tm, MEM_W), F32)],
        compiler_params=pltpu.CompilerParams(dimension_semantics=("arbitrary",), vmem_limit_bytes=VMEM_LIMIT),
        name=name,
    )(x2, om2, mk, mv, *consts)


def _decode_kernel(pt_ref, qm_ref, fq_ref, cum_ref, locm_ref, lock_ref, locv_ref, locf_ref,
                   wkt_ref, wv_ref, gkn_ref, u_ref, linc_ref, cm_hbm, ck_hbm, cv_hbm, cl_hbm, o_ref,
                   mla_buf, fk_buf, fv_buf, lf_buf, sems, lhs_s, qr_s, qbd_s, carry_s, m1, l1, ctx, m2, l2, acc2,
                   *pend, nj, n_pages, n_seq):
    b = pl.program_id(0)
    j = pl.program_id(1)
    step = b * nj + j
    nrow = MLA_HEADS * DEC_T
    pend_a, pend_b = pend[0:4], pend[4:8]

    def page_copies(seq, jj, to_slot):
        base = seq * n_pages + (nj - 1 - jj) * PPS
        copies = []
        for i in range(PPS):
            page = pt_ref[base + i]
            for k, (src, dst) in enumerate(((cm_hbm, mla_buf), (ck_hbm, fk_buf), (cv_hbm, fv_buf), (cl_hbm, lf_buf))):
                copies.append(pltpu.make_async_copy(src.at[page], dst.at[to_slot, i], sems.at[to_slot, k]))
        return copies

    @pl.when(step == 0)
    def _():
        for c in page_copies(0, 0, 0):
            c.start()

    @pl.when(step + 1 < n_seq * nj)
    def _():
        wrap = j == nj - 1
        for c in page_copies(jnp.where(wrap, b + 1, b), jnp.where(wrap, 0, j + 1), (step + 1) % 2):
            c.start()

    for c in page_copies(b, j, step % 2):
        c.wait()
    cum8 = cum_ref[0]

    def mla_scores(lat_t, rope_t):
        big = _dot(lhs_s[...], lat_t)
        rs = _dot(qr_s[...], rope_t)
        slabs = []
        for h in range(MLA_HEADS):
            kvh = big[h * NOPE:(h + 1) * NOPE]
            r = lax.rsqrt(jnp.sum(kvh * kvh, axis=0, keepdims=True) * (1.0 / NOPE) + EPS)
            base = MLA_HEADS * NOPE + h * DEC_T
            slabs.append(big[base:base + DEC_T] * r + rs[h * DEC_T:(h + 1) * DEC_T])
        return jnp.concatenate(slabs, axis=0)

    def fox_scores(k_t, bias8):
        s = _dot(qbd_s[...], k_t)
        slabs = []
        for h in range(FOX_HEADS):
            slabs.append(s[h * DEC_T:(h + 1) * DEC_T] + bias8[h:h + 1, :] + cum8[:, h:h + 1])
        return jnp.concatenate(slabs, axis=0)

    def update(s, m_ref, l_ref, acc_ref, vals_t):
        m_old = m_ref[...]
        m_new = jnp.maximum(m_old, jnp.max(s, axis=1, keepdims=True))
        alpha = jnp.exp(m_old - m_new)
        p = jnp.exp(s - m_new)
        l_ref[...] = alpha * l_ref[...] + jnp.sum(p, axis=1, keepdims=True)
        m_ref[...] = m_new
        acc_ref[...] = alpha * acc_ref[...] + _dot_nt(p.astype(BF16), vals_t)

    @pl.when(j == 0)
    def _():
        qm8 = qm_ref[0].astype(F32)
        qn = jnp.concatenate([qm8[:, (2 * p) * SLAB:(2 * p) * SLAB + LANE]
                              + qm8[:, (2 * p + 1) * SLAB:(2 * p + 1) * SLAB + LANE] for p in range(4)], axis=1)
        qn = qn * gkn_ref[...]
        head512 = lax.broadcasted_iota(jnp.int32, (1, MLA_HEADS * NOPE), 1) // NOPE
        qn_bd = jnp.concatenate([jnp.where(head512 == h, qn, 0.0) for h in range(MLA_HEADS)], axis=0)
        lhs_s[0:MLA_HEADS * NOPE, :] = wkt_ref[...]
        lhs_s[MLA_HEADS * NOPE:, :] = _dot(qn_bd.astype(BF16), wkt_ref[...]).astype(BF16)
        qr_s[...] = jnp.concatenate([qm8[:, h * SLAB + LANE:h * SLAB + LANE + ROPE]
                                     for h in range(MLA_HEADS)], axis=0).astype(BF16)
        fq8 = fq_ref[0].astype(F32)
        zero = jnp.zeros((DEC_T, LANE), F32)
        blocks = []
        for h in range(FOX_HEADS):
            slab = fq8[:, h * SLAB:h * SLAB + LANE]
            blocks.append(jnp.concatenate([slab, zero] if (h // 2) // 2 == 0 else [zero, slab], axis=1))
        qbd_s[...] = jnp.concatenate(blocks, axis=0).astype(BF16)
        carry_s[...] = jnp.zeros_like(carry_s)
        m1[...] = jnp.full(m1.shape, NEG, F32)
        m2[...] = jnp.full(m2.shape, NEG, F32)
        l1[...] = jnp.zeros_like(l1)
        l2[...] = jnp.zeros_like(l2)
        ctx[...] = jnp.zeros_like(ctx)
        acc2[...] = jnp.zeros_like(acc2)

        tok = lax.broadcasted_iota(jnp.int32, (nrow, LANE), 0) % DEC_T
        valid = lax.broadcasted_iota(jnp.int32, (nrow, LANE), 1) <= tok
        lat_t = locm_ref[0, 0:KV_LORA, :].astype(BF16)
        s1 = mla_scores(lat_t, locm_ref[0, KV_LORA:MLA_ROW, :].astype(BF16))
        cum_t = _dot3(locf_ref[0], linc_ref[...])
        s2 = fox_scores(lock_ref[0].astype(BF16), -cum_t)
        s1_p, s2_p, lat_p, v_p = pend_b
        s1_p[...] = jnp.full(s1_p.shape, NEG, F32)
        s2_p[...] = jnp.full(s2_p.shape, NEG, F32)
        lat_p[...] = jnp.zeros_like(lat_p)
        v_p[...] = jnp.zeros_like(v_p)
        s1_p[:, 0:LANE] = jnp.where(valid, s1, NEG)
        s2_p[:, 0:LANE] = jnp.where(valid, s2, NEG)
        lat_p[:, 0:LANE] = lat_t
        v_p[:, 0:LANE] = locv_ref[0].astype(BF16)

    def score_phase(sl, queue):
        s1_p, s2_p, lat_p, v_p = queue
        lat_t = jnp.concatenate([mla_buf[sl, i, 0:KV_LORA, :].astype(BF16) for i in range(PPS)], axis=1)
        rope_t = jnp.concatenate([mla_buf[sl, i, KV_LORA:MLA_ROW, :].astype(BF16) for i in range(PPS)], axis=1)
        s1_p[...] = mla_scores(lat_t, rope_t)
        lat_p[...] = lat_t
        lfs = jnp.concatenate([lf_buf[sl, i] for i in range(PPS)], axis=0)
        res = _dot3(lfs, u_ref[...])
        carry = carry_s[...]
        biases = [None] * PPS
        for i in reversed(range(PPS)):
            biases[i] = res[i * 8:(i + 1) * 8, 0:LANE] + carry
            carry = carry + res[i * 8:(i + 1) * 8, LANE:2 * LANE]
        carry_s[...] = carry
        k_t = jnp.concatenate([fk_buf[sl, i].astype(BF16) for i in range(PPS)], axis=1)
        s2_p[...] = fox_scores(k_t, jnp.concatenate(biases, axis=1))
        v_p[...] = jnp.concatenate([fv_buf[sl, i].astype(BF16) for i in range(PPS)], axis=1)

    def value_phase(queue):
        s1_p, s2_p, lat_p, v_p = queue
        update(s1_p[...], m1, l1, ctx, lat_p[...])
        update(s2_p[...], m2, l2, acc2, v_p[...])

    @pl.when(j % 2 == 0)
    def _():
        score_phase(0, pend_a)
        value_phase(pend_b)

    @pl.when(j % 2 == 1)
    def _():
        score_phase(1, pend_b)
        value_phase(pend_a)

    @pl.when(j == nj - 1)
    def _():
        value_phase(pend_b)
        full = _dot((ctx[...] * (1.0 / l1[...])).astype(BF16), wv_ref[...])
        head512 = lax.broadcasted_iota(jnp.int32, (1, MLA_HEADS * MLA_V), 1) // MLA_V
        o_mla = jnp.where(head512 == 0, full[0:DEC_T], 0.0)
        for h in range(1, MLA_HEADS):
            o_mla = o_mla + jnp.where(head512 == h, full[h * DEC_T:(h + 1) * DEC_T], 0.0)
        a2 = acc2[...] * (1.0 / l2[...])
        head256 = lax.broadcasted_iota(jnp.int32, (1, FOX_KV * FOX_DIM), 1) // FOX_DIM
        outs = []
        for grp in range(FOX_HEADS // FOX_KV):
            o = None
            for kvh in range(FOX_KV):
                r0 = (kvh * 2 + grp) * DEC_T
                term = jnp.where(head256 == kvh, a2[r0:r0 + DEC_T], 0.0)
                o = term if o is None else o + term
            outs.append(o)
        o_ref[0] = jnp.concatenate([o_mla] + outs, axis=1)


def _decode(page_table_flat, qm, fq, cum, locm, lock, locv, locf, cm, ck, cv, cl, w, n_seq, n_pages):
    nj = n_pages // PPS
    assert nj % 2 == 0, "buffer slots alternate with the step parity within a sequence"
    keys = PPS * LANE

    seq = lambda rows, width: pl.BlockSpec((1, rows, width), lambda b, j, pt: (b, 0, 0))
    cspec = lambda a: pl.BlockSpec(a.shape, lambda b, j, pt: (0,) * a.ndim)
    consts = [w['w_kt'], w['w_v'], w['g_kn512'], w['u_mat'], w['l_inc']]
    nrow = MLA_HEADS * DEC_T
    grid_spec = pltpu.PrefetchScalarGridSpec(
        num_scalar_prefetch=1,
        grid=(n_seq, nj),
        in_specs=[seq(DEC_T, MLA_HEADS * SLAB), seq(DEC_T, FOX_HEADS * SLAB), seq(DEC_T, LANE),
                  seq(MLA_ROW, LANE), seq(FOX_KV * FOX_DIM, LANE), seq(FOX_KV * FOX_DIM, LANE), seq(FOX_HEADS, LANE)]
        + [cspec(c) for c in consts] + [pl.BlockSpec(memory_space=pl.ANY)] * 4,
        out_specs=pl.BlockSpec((1, DEC_T, D_MODEL), lambda b, j, pt: (b, 0, 0)),
        scratch_shapes=[pltpu.VMEM((2, PPS, MLA_ROW, LANE), F32), pltpu.VMEM((2, PPS, FOX_KV * FOX_DIM, LANE), F32),
                        pltpu.VMEM((2, PPS, FOX_KV * FOX_DIM, LANE), F32), pltpu.VMEM((2, PPS, FOX_HEADS, LANE), F32),
                        pltpu.SemaphoreType.DMA((2, 4)),
                        pltpu.VMEM((MLA_HEADS * NOPE + nrow, KV_LORA), BF16), pltpu.VMEM((nrow, ROPE), BF16),
                        pltpu.VMEM((nrow, FOX_KV * FOX_DIM), BF16), pltpu.VMEM((FOX_HEADS, LANE), F32),
                        pltpu.VMEM((nrow, 1), F32), pltpu.VMEM((nrow, 1), F32), pltpu.VMEM((nrow, KV_LORA), F32),
                        pltpu.VMEM((nrow, 1), F32), pltpu.VMEM((nrow, 1), F32),
                        pltpu.VMEM((nrow, FOX_KV * FOX_DIM), F32)]
        + [pltpu.VMEM((nrow, keys), F32), pltpu.VMEM((nrow, keys), F32), pltpu.VMEM((KV_LORA, keys), BF16),
           pltpu.VMEM((FOX_KV * FOX_DIM, keys), BF16)] * 2)
    return pl.pallas_call(
        functools.partial(_decode_kernel, nj=nj, n_pages=n_pages, n_seq=n_seq),
        grid_spec=grid_spec,
        out_shape=jax.ShapeDtypeStruct((n_seq, DEC_T, D_MODEL), F32),
        compiler_params=pltpu.CompilerParams(dimension_semantics=("arbitrary", "arbitrary"),
                                             vmem_limit_bytes=VMEM_LIMIT),
        name="decode",
    )(page_table_flat, qm, fq, cum, locm, lock, locv, locf, *consts, cm, ck, cv, cl)


def _place(vec, offset, width=LANE):
    return jnp.zeros((width,), F32).at[offset:offset + vec.shape[0]].set(vec)


def _rot_half(v, axis=-1):
    a, b = jnp.split(v, 2, axis=axis)
    return jnp.concatenate([-b, a], axis=axis)


def _prep_weights(g_mix, w_in, g_q_lat, w_q_up, g_kv_lat, g_k_rope, g_q_nope, g_q_rope, g_k_nope, w_kv_up, g_fox_q,
                  g_fox_k, b_forget, w_o, g_cross, g_mem, w_mq, w_mk, w_mv, g_mq, g_mk, w_mo, g_ffn, w_up, w_down):
    w = {}
    win = w_in[0]
    seg = [win[:, IN_SPLITS[k]:IN_SPLITS[k + 1]] for k in range(7)]
    c_q, c_kv, k_r, f_q, f_k, f_v, f_l = seg
    zc = lambda n: jnp.zeros((D_MODEL, n), F32)
    cols = [c_q, c_kv, k_r, zc(LANE - ROPE), _rot_half(k_r), zc(LANE - ROPE)]
    for h in range(FOX_HEADS):
        off = ((h // 2) % 2) * FOX_DIM
        cols += [zc(off), f_q[:, h * FOX_DIM:(h + 1) * FOX_DIM], zc(LANE - FOX_DIM - off)]
    cols += [f_k, f_v, f_l, zc(LANE - FOX_HEADS)]
    w['w_in'] = jnp.concatenate([c for c in cols if c.shape[1]], axis=1).astype(BF16)

    wq = w_q_up[0]
    zq = lambda n: jnp.zeros((Q_LORA, n), F32)
    qcols, rcols = [], []
    for h in range(MLA_HEADS):
        base = h * (NOPE + ROPE)
        off = (h % 2) * NOPE
        rope_w = wq[:, base + NOPE:base + NOPE + ROPE]
        qcols += [zq(off), wq[:, base:base + NOPE], zq(LANE - NOPE - off), rope_w, zq(LANE - ROPE)]
        rcols += [_rot_half(rope_w), zq(LANE - ROPE)]
    w['w_q'] = jnp.concatenate([c for c in qcols + rcols if c.shape[1]], axis=1).astype(BF16)

    wkv = w_kv_up[0].reshape(KV_LORA, MLA_HEADS, NOPE + MLA_V)
    w_k = wkv[:, :, :NOPE].reshape(KV_LORA, MLA_HEADS * NOPE)
    w_v = wkv[:, :, NOPE:].reshape(KV_LORA, MLA_HEADS * MLA_V)
    w['w_kv'] = jnp.concatenate([w_k, w_v], axis=1).astype(BF16)
    w['w_kt'] = w_k.T.astype(BF16)
    w['w_v'] = w_v.astype(BF16)
    w['g_kn512'] = jnp.tile(g_k_nope[0], MLA_HEADS)[None, :]

    gqr, gkr = g_q_rope[0] * MLA_SCALE, g_k_rope[0]
    swap = lambda v: jnp.concatenate([v[HALF:], v[:HALF]])
    k_aug_base = jnp.zeros((LANE,), F32).at[0:3].set(1.0)
    rows = [_place(g_q_nope[0] * MLA_SCALE, 0), _place(g_q_nope[0] * MLA_SCALE, NOPE), _place(gqr, 0),
            _place(swap(gqr), 0), _place(gkr, 0), _place(swap(gkr), 0), jnp.tile(g_k_nope[0], 2),
            _place(g_fox_q[0] * FOX_SCALE, 0), _place(g_fox_q[0] * FOX_SCALE, FOX_DIM), jnp.tile(g_fox_k[0], 2),
            _place(b_forget[0], 0), k_aug_base]
    rows += [jnp.zeros((LANE,), F32)] * (16 - len(rows))
    w['g128'] = jnp.stack(rows)

    place = np.zeros((3, LANE, LANE + FOX_HEADS * LANE), np.float32)
    base_q = np.zeros((1, FOX_HEADS * LANE), np.float32)
    for piece in range(3):
        for h in range(FOX_HEADS):
            place[piece, h, 3 + 3 * h + piece] = -1.0
            place[piece, h, LANE + h * LANE + piece] = 1.0
            base_q[0, h * LANE + 3 + 3 * h + piece] = 1.0
    w['place'] = jnp.asarray(place.reshape(3 * LANE, -1), BF16)
    w['base_q'] = jnp.asarray(base_q)

    w['g_mix'], w['g_q_lat'], w['g_kv_lat'] = g_mix, g_q_lat, g_kv_lat
    w['g_cross'], w['g_mem'], w['g_ffn'] = g_cross, g_mem, g_ffn
    w['g_mq'] = jnp.tile(g_mq[0] * MEM_SCALE, MEM_HEADS)[None, :]
    w['g_mk'] = jnp.tile(g_mk[0], MEM_HEADS)[None, :]
    w['w_mkv'] = jnp.concatenate([w_mk[0], w_mv[0]], axis=1).astype(BF16)
    w['w_mq'] = w_mq[0].astype(BF16)
    w['w_mo'] = w_mo[0].astype(BF16)
    w['w_up'] = w_up[0].astype(BF16)
    w['w_down'] = w_down[0].astype(BF16)

    wo = w_o[0]
    mla_w = MLA_HEADS * MLA_V
    fox_rows = wo[mla_w:].reshape(FOX_KV // 2, 2, 2, FOX_DIM, D_MODEL)
    prompt_fox = jnp.transpose(fox_rows, (0, 2, 1, 3, 4)).reshape(FOX_HEADS * FOX_DIM, D_MODEL)
    w['w_o'] = jnp.concatenate([wo[:mla_w], prompt_fox], axis=0).astype(BF16)
    fox_rows = wo[mla_w:].reshape(FOX_KV, 2, FOX_DIM, D_MODEL)
    sample_fox = jnp.transpose(fox_rows, (1, 0, 2, 3)).reshape(FOX_HEADS * FOX_DIM, D_MODEL)
    w['w_o_sample'] = jnp.concatenate([wo[:mla_w], sample_fox], axis=0).astype(BF16)

    idx = np.arange(LANE)
    u_strict = (idx[:, None] > idx[None, :]).astype(np.float32)
    w['u_mat'] = jnp.asarray(np.concatenate([u_strict, np.ones((LANE, LANE), np.float32)], axis=1), BF16)
    w['l_inc'] = jnp.asarray((idx[:, None] <= idx[None, :]).astype(np.float32), BF16)
    return w


def _rope_tables(pos):
    inv_freq = ROPE_BASE ** (-jnp.arange(HALF, dtype=F32) / HALF)
    ang = pos.astype(F32)[:, None] * inv_freq[None, :]
    pad = jnp.zeros((pos.shape[0], LANE - ROPE), F32)
    cos = jnp.concatenate([jnp.cos(ang), jnp.cos(ang), pad], axis=1)
    sin = jnp.concatenate([jnp.sin(ang), jnp.sin(ang), pad], axis=1)
    return cos, sin


def _tri(tm, group):
    idx = np.arange(tm)
    m = (idx[None, :] <= idx[:, None]) & (idx[None, :] // group == idx[:, None] // group)
    return jnp.asarray(m.astype(np.float32), BF16)


def kernel(x_prompt, x_sample, cache_mla, cache_fox_k, cache_fox_v, cache_fox_logf, cache_mem_k, cache_mem_v,
           page_table, mem_prompt, g_mix, w_in, g_q_lat, w_q_up, g_kv_lat, g_k_rope, g_q_nope, g_q_rope, g_k_nope,
           w_kv_up, g_fox_q, g_fox_k, b_forget, w_o, g_cross, g_mem, w_mq, w_mk, w_mv, g_mq, g_mk, w_mo, g_ffn, w_up,
           w_down):
    assert cache_mla.shape[0] == 1, "single-layer kernel"
    batch, s_len, _ = x_prompt.shape
    n_seq, t_new, _ = x_sample.shape
    n_pool, page = cache_mla.shape[1], cache_mla.shape[2]
    n_pages = page_table.shape[1]
    mem_len = mem_prompt.shape[1]
    assert page == LANE and t_new <= DEC_T and n_pages % PPS == 0
    assert s_len % PROJ_TM == 0 and s_len % ATTN_T == 0 and s_len % FIN_TM == 0
    w = _prep_weights(g_mix, w_in, g_q_lat, w_q_up, g_kv_lat, g_k_rope, g_q_nope, g_q_rope, g_k_nope, w_kv_up,
                      g_fox_q, g_fox_k, b_forget, w_o, g_cross, g_mem, w_mq, w_mk, w_mv, g_mq, g_mk, w_mo, g_ffn,
                      w_up, w_down)

    n = batch * s_len
    xp = x_prompt.reshape(n, D_MODEL)
    cos_p, sin_p = _rope_tables(jnp.arange(s_len))
    (p_row_t, qm, kn, vm_t, kr, fq, fk_t, fkb, fv_t, fvb_t, p_lf_t, _, kaug) = _proj(
        xp, cos_p, sin_p, _tri(PROJ_TM, PROJ_TM), w, s_len // PROJ_TM)
    o_mla = _attention(qm, kn, kr, vm_t, batch, s_len, 2, ((0, 1),), "attn_mla")
    o_fox = _attention(fq, fkb, kaug, fvb_t, batch, s_len, 4, ((0, 2), (1, 3)), "attn_fox")
    mk_t, mv_t = _memkv(mem_prompt.reshape(batch * mem_len, D_MODEL), batch, mem_len, w)
    y_prompt = _finish(xp, jnp.concatenate([o_mla, o_fox], axis=1), mk_t, mv_t, w['w_o'], w, 1, FIN_TM,
                       s_len // FIN_TM, "finish_prompt")

    ns = n_seq * DEC_T
    xs = jnp.pad(x_sample, ((0, 0), (0, DEC_T - t_new), (0, 0))).reshape(ns, D_MODEL)
    pos_s = n_pages * page + (jnp.arange(PROJ_TM) % DEC_T)
    cos_s, sin_s = _rope_tables(pos_s)
    (s_row_t, qm_s, _, _, _, fq_s, fk_s_t, _, fv_s_t, _, s_lf_t, cum_s, _) = _proj(
        xs, cos_s, sin_s, _tri(PROJ_TM, DEC_T), w, 1)

    def new_tokens(a_t):
        feat = a_t.shape[0] // (ns // PROJ_TM)
        a = jnp.transpose(a_t.reshape(ns // PROJ_TM, feat, PROJ_TM), (1, 0, 2))
        return a.reshape(feat, n_seq, DEC_T)[:, :, :t_new]

    def new_page(a_t):
        return jnp.pad(jnp.transpose(new_tokens(a_t), (1, 0, 2)), ((0, 0), (0, 0), (0, LANE - t_new)))

    cm = jnp.transpose(cache_mla[0], (0, 2, 1))
    ck = jnp.transpose(cache_fox_k[0], (0, 2, 3, 1)).reshape(n_pool, FOX_KV * FOX_DIM, page)
    cv = jnp.transpose(cache_fox_v[0], (0, 2, 3, 1)).reshape(n_pool, FOX_KV * FOX_DIM, page)
    cl = jnp.transpose(cache_fox_logf[0], (0, 2, 1))
    o_s = _decode(page_table.reshape(-1), qm_s.reshape(n_seq, DEC_T, -1), fq_s.reshape(n_seq, DEC_T, -1),
                  cum_s.reshape(n_seq, DEC_T, LANE), new_page(s_row_t), new_page(fk_s_t), new_page(fv_s_t),
                  new_page(s_lf_t), cm, ck, cv, cl, w, n_seq, n_pages)
    cmk = jnp.transpose(cache_mem_k[0], (0, 2, 3, 1)).reshape(n_seq, MEM_W, mem_len)
    cmv = jnp.transpose(cache_mem_v[0], (0, 2, 3, 1)).reshape(n_seq, MEM_W, mem_len)
    y_s = _finish(xs, o_s.reshape(ns, D_MODEL), cmk, cmv, w['w_o_sample'], w, 16, DEC_T, 1, "finish_sample")

    def tok(a_t, *shape):
        return jnp.transpose(new_tokens(a_t), (1, 2, 0)).reshape(1, n_seq, t_new, *shape)

    def seq_out(a_t, *shape):
        return jnp.transpose(a_t.reshape(batch, -1, a_t.shape[1]), (0, 2, 1)).reshape(1, batch, a_t.shape[1], *shape)

    y_sample = y_s.reshape(n_seq, DEC_T, D_MODEL)[:, :t_new]
    return (y_prompt.reshape(batch, s_len, D_MODEL), y_sample,
            seq_out(p_row_t, MLA_ROW), seq_out(fk_t, FOX_KV, FOX_DIM), seq_out(fv_t, FOX_KV, FOX_DIM),
            seq_out(p_lf_t, FOX_HEADS), seq_out(mk_t.reshape(batch * MEM_W, mem_len), MEM_HEADS, MEM_DIM),
            seq_out(mv_t.reshape(batch * MEM_W, mem_len), MEM_HEADS, MEM_DIM),
            tok(s_row_t, MLA_ROW), tok(fk_s_t, FOX_KV, FOX_DIM), tok(fv_s_t, FOX_KV, FOX_DIM),
            tok(s_lf_t, FOX_HEADS))
```

```python
import functools

import numpy as np
import jax
import jax.numpy as jnp
from jax import lax
from jax.experimental import pallas as pl
from jax.experimental.pallas import tpu as pltpu

F32 = jnp.float32
BF16 = jnp.bfloat16

D_MODEL = 1024
Q_LORA = 384
KV_LORA = 256
ROPE = 32
HALF = ROPE // 2
NOPE = 64
MLA_V = 64
MLA_HEADS = 8
MLA_ROW = KV_LORA + ROPE
FOX_HEADS = 8
FOX_KV = 4
FOX_DIM = 64
MEM_HEADS = 4
MEM_DIM = 64
MEM_W = MEM_HEADS * MEM_DIM
D_FF = 4096
IN_SPLITS = (0, 384, 640, 672, 1184, 1440, 1696, 1704)
ROPE_BASE = 10000.0
EPS = 1e-6
NEG = -1e30
MLA_SCALE = (NOPE + ROPE) ** -0.5
FOX_SCALE = FOX_DIM ** -0.5
MEM_SCALE = MEM_DIM ** -0.5

LANE = 128
SLAB = 2 * LANE
VMEM_LIMIT = 56 * 1024 * 1024
PROJ_TM = 256
ATTN_T = 512
FIN_TM = 1024
DEC_T = 8
PPS = 32
FFN_CHUNK = 1024
ONES_ROWS = 16

C_CQ, C_CKV, C_KR, C_KROT, C_FQ, C_FK, C_FV, C_FL, W_IN_EXP = 0, 384, 640, 768, 896, 1920, 2176, 2432, 2560
G_QN_E, G_QN_O, G_QR, G_QR_ROT, G_KR, G_KR_ROT, G_KN, G_FQ_E, G_FQ_O, G_FK, G_BF, G_KAUG = range(12)


def _const_spec(shape):
    n = len(shape)
    return pl.BlockSpec(shape, lambda *_: (0,) * n, pipeline_mode=pl.Buffered(1))


def _rms(x, width):
    return x * lax.rsqrt(jnp.sum(x * x, axis=-1, keepdims=True) * (1.0 / width) + EPS)


def _group64_rms(x):
    out = []
    lane = lax.broadcasted_iota(jnp.int32, (1, LANE), 1)
    lo = lane < 64
    for j in range(x.shape[1] // LANE):
        s = x[:, j * LANE:(j + 1) * LANE]
        s2 = s * s
        r_lo = lax.rsqrt(jnp.sum(jnp.where(lo, s2, 0.0), axis=-1, keepdims=True) * (1.0 / 64) + EPS)
        r_hi = lax.rsqrt(jnp.sum(jnp.where(lo, 0.0, s2), axis=-1, keepdims=True) * (1.0 / 64) + EPS)
        out.append(s * jnp.where(lo, r_lo, r_hi))
    return out[0] if len(out) == 1 else jnp.concatenate(out, axis=1)


def _split3(x):
    hi = x.astype(BF16)
    r1 = x - hi.astype(F32)
    mid = r1.astype(BF16)
    lo = (r1 - mid.astype(F32)).astype(BF16)
    return hi, mid, lo


def _dot(a, b):
    return jnp.dot(a, b, preferred_element_type=F32)


def _dot_nt(a, b):
    return lax.dot_general(a, b, (((1,), (1,)), ((), ())), preferred_element_type=F32)


def _dot3(x, w):
    hi, mid, lo = _split3(x)
    return _dot(hi, w) + _dot(mid, w) + _dot(lo, w)


def _proj_kernel(x_ref, cos_ref, sin_ref, tri_ref, gmix_ref, win_ref, gql_ref, wq_ref, gkv_ref, wkv_ref,
                 g128_ref, place_ref, baseq_ref,
                 rowt_ref, qm_ref, kn_ref, vmt_ref, kr_ref, fq_ref, fkt_ref, fkb_ref, fvt_ref, fvbt_ref,
                 lft_ref, cum_ref, kaug_ref, carry_ref, *, tiles_per_seq):
    i = pl.program_id(0)
    cos = cos_ref[...]
    sin = sin_ref[...]

    def g(r):
        return g128_ref[r:r + 1, :]

    xn = (_rms(x_ref[...], D_MODEL) * gmix_ref[...]).astype(BF16)
    a = _dot(xn, win_ref[...])

    cq = (_rms(a[:, C_CQ:C_CQ + Q_LORA], Q_LORA) * gql_ref[...]).astype(BF16)
    q = _dot(cq, wq_ref[...])
    for h in range(MLA_HEADS):
        nope = q[:, h * SLAB:h * SLAB + LANE]
        rope = q[:, h * SLAB + LANE:(h + 1) * SLAB]
        rot = q[:, MLA_HEADS * SLAB + h * LANE:MLA_HEADS * SLAB + (h + 1) * LANE]
        rn = lax.rsqrt(jnp.sum(nope * nope, axis=-1, keepdims=True) * (1.0 / NOPE) + EPS)
        rr = lax.rsqrt(jnp.sum(rope * rope, axis=-1, keepdims=True) * (1.0 / ROPE) + EPS)
        qm_ref[:, h * SLAB:h * SLAB + LANE] = (nope * rn * g(G_QN_O if h % 2 else G_QN_E)).astype(BF16)
        qm_ref[:, h * SLAB + LANE:(h + 1) * SLAB] = (
            rr * (rope * g(G_QR) * cos + rot * g(G_QR_ROT) * sin)).astype(BF16)

    lat = _rms(a[:, C_CKV:C_CKV + KV_LORA], KV_LORA) * gkv_ref[...]
    kr = a[:, C_KR:C_KR + LANE]
    krot = a[:, C_KROT:C_KROT + LANE]
    rk = lax.rsqrt(jnp.sum(kr * kr, axis=-1, keepdims=True) * (1.0 / ROPE) + EPS)
    k_rope = rk * (kr * g(G_KR) * cos + krot * g(G_KR_ROT) * sin)
    rowt_ref[0:KV_LORA, :] = lat.T
    rowt_ref[KV_LORA:MLA_ROW, :] = k_rope.T[0:ROPE]
    kr_ref[...] = k_rope.astype(BF16)

    kv = _dot(lat.astype(BF16), wkv_ref[...])
    gkn = g(G_KN)
    kn = _group64_rms(kv[:, 0:MLA_HEADS * NOPE])
    kn_ref[...] = (kn * jnp.concatenate([gkn] * 4, axis=1)).astype(BF16)
    vmt_ref[...] = kv[:, MLA_HEADS * NOPE:].T.astype(BF16)

    for h in range(FOX_HEADS):
        s = a[:, C_FQ + h * LANE:C_FQ + (h + 1) * LANE]
        r = lax.rsqrt(jnp.sum(s * s, axis=-1, keepdims=True) * (1.0 / FOX_DIM) + EPS)
        fq_ref[:, h * SLAB:h * SLAB + LANE] = (s * r * g(G_FQ_O if (h // 2) % 2 else G_FQ_E)).astype(BF16)
    fk = _group64_rms(a[:, C_FK:C_FK + 2 * LANE]) * jnp.concatenate([g(G_FK)] * 2, axis=1)
    fkt_ref[...] = fk.T
    fkb_ref[...] = fk.astype(BF16)
    fvt = a[:, C_FV:C_FV + 2 * LANE].T
    fvt_ref[...] = fvt
    fvbt_ref[...] = fvt.astype(BF16)

    z = a[:, C_FL:C_FL + LANE] + g(G_BF)
    lane = lax.broadcasted_iota(jnp.int32, (1, LANE), 1)
    logf = jnp.where(lane < FOX_HEADS, jnp.minimum(z, 0.0) - jnp.log1p(jnp.exp(-jnp.abs(z))), 0.0)
    lft_ref[...] = logf.T[0:FOX_HEADS]

    @pl.when(i % tiles_per_seq == 0)
    def _():
        carry_ref[...] = jnp.zeros_like(carry_ref)

    l_hi, l_mid, l_lo = _split3(logf)
    tri = tri_ref[...]
    parts = _dot(tri, jnp.concatenate([l_hi, l_mid, l_lo], axis=1))
    cum = parts[:, 0:LANE] + parts[:, LANE:2 * LANE] + parts[:, 2 * LANE:] + carry_ref[0:1, :]
    cum_ref[...] = cum
    tm = cum.shape[0]
    carry_ref[...] = jnp.broadcast_to(cum[tm - 1:tm, :], carry_ref.shape)

    c_hi, c_mid, c_lo = _split3(cum)
    aug = _dot(jnp.concatenate([c_hi, c_mid, c_lo], axis=1), place_ref[...])
    kaug_ref[...] = (aug[:, 0:LANE] + g(G_KAUG)).astype(BF16)
    for h in range(FOX_HEADS):
        fq_ref[:, h * SLAB + LANE:(h + 1) * SLAB] = (
            aug[:, (h + 1) * LANE:(h + 2) * LANE] + baseq_ref[:, h * LANE:(h + 1) * LANE]).astype(BF16)


def _proj(x2, cos_t, sin_t, tri, w, tiles_per_seq):
    n = x2.shape[0]
    tm = PROJ_TM
    n_tab = cos_t.shape[0] // tm
    n_seq = n // (tiles_per_seq * tm)
    s_len = tiles_per_seq * tm
    row = lambda width: pl.BlockSpec((tm, width), lambda i: (i, 0))
    tab = pl.BlockSpec((tm, LANE), lambda i: (i % n_tab, 0))
    outs = [('t', MLA_ROW, F32), ('r', MLA_HEADS * SLAB, BF16), ('r', MLA_HEADS * NOPE, BF16),
            ('t', MLA_HEADS * MLA_V, BF16), ('r', LANE, BF16), ('r', FOX_HEADS * SLAB, BF16),
            ('t', FOX_KV * FOX_DIM, F32), ('r', FOX_KV * FOX_DIM, BF16), ('t', FOX_KV * FOX_DIM, F32),
            ('t', FOX_KV * FOX_DIM, BF16), ('t', FOX_HEADS, F32), ('r', LANE, F32), ('r', LANE, BF16)]
    out_specs, out_shape = [], []
    for kind, width, dtype in outs:
        if kind == 'r':
            out_specs.append(row(width))
            out_shape.append(jax.ShapeDtypeStruct((n, width), dtype))
        else:
            out_specs.append(pl.BlockSpec((width, tm), lambda i: (i // tiles_per_seq, i % tiles_per_seq)))
            out_shape.append(jax.ShapeDtypeStruct((n_seq * width, s_len), dtype))
    consts = [w['g_mix'], w['w_in'], w['g_q_lat'], w['w_q'], w['g_kv_lat'], w['w_kv'], w['g128'], w['place'],
              w['base_q']]
    return pl.pallas_call(
        functools.partial(_proj_kernel, tiles_per_seq=tiles_per_seq),
        grid=(n // tm,),
        in_specs=[row(D_MODEL), tab, tab, _const_spec(tri.shape)] + [_const_spec(c.shape) for c in consts],
        out_specs=out_specs,
        out_shape=out_shape,
        scratch_shapes=[pltpu.VMEM((8, LANE), F32)],
        compiler_params=pltpu.CompilerParams(dimension_semantics=("arbitrary",), vmem_limit_bytes=VMEM_LIMIT),
        name="proj",
    )(x2, cos_t, sin_t, tri, *consts)


def _attn_kernel(q_ref, km_ref, ke_ref, vt_ref, o_ref, qt_s, qt_s2, s_a, s_b, m_s, acc_s, *, hb, t, s_len, combos):
    cols = hb * t
    key_i = lax.broadcasted_iota(jnp.int32, (t, cols), 0)
    causal = key_i <= lax.broadcasted_iota(jnp.int32, (t, cols), 1) % t
    lo = lax.broadcasted_iota(jnp.int32, (LANE, 1), 0) < 64
    ones = jnp.ones((ONES_ROWS, t), BF16)

    qts = (qt_s, qt_s2)
    bufs = (s_a, s_b)

    def prepare_q(qb):
        r0 = qb * t
        qts[qb % 2][...] = jnp.concatenate(
            [q_ref[r0:r0 + t, h * SLAB:(h + 1) * SLAB].astype(F32).T for h in range(hb)], axis=1).astype(BF16)

    def scores(qb, kb, buf):
        kc = jnp.concatenate([km_ref[kb * t:(kb + 1) * t, :], ke_ref[kb * t:(kb + 1) * t, :]], axis=1)
        buf[...] = _dot(kc, qts[qb % 2][...])

    def update(buf, kb, masked):
        s = buf[...]
        if masked:
            s = jnp.where(causal, s, NEG)
        m = m_s[...]
        m_new = jnp.maximum(m, jnp.max(s, axis=0, keepdims=True))
        alpha = jnp.exp(m - m_new)
        p = jnp.exp(s - m_new).astype(BF16)
        vt1 = jnp.concatenate([vt_ref[:, kb * t:(kb + 1) * t], ones], axis=0)
        m_s[...] = m_new
        acc_s[...] = alpha * acc_s[...] + _dot(vt1, p)

    def finish_q(qb):
        acc = acc_s[...]
        o = acc[0:LANE] * (1.0 / acc[LANE:LANE + 1])
        out = [jnp.where(lo, o[:, a * t:(a + 1) * t], o[:, b * t:(b + 1) * t]).T for a, b in combos]
        o_ref[qb * t:(qb + 1) * t, :] = jnp.concatenate(out, axis=1).astype(o_ref.dtype)

    blocks = [(qb, kb) for qb in range(s_len // t) for kb in range(qb + 1)]
    prepare_q(0)
    scores(0, 0, bufs[0])
    for n, (qb, kb) in enumerate(blocks):
        if n + 1 < len(blocks):
            nqb, nkb = blocks[n + 1]
            if nkb == 0:
                prepare_q(nqb)
            scores(nqb, nkb, bufs[(n + 1) % 2])
        if kb == 0:
            m_s[...] = jnp.full(m_s.shape, NEG, F32)
            acc_s[...] = jnp.zeros_like(acc_s)
        update(bufs[n % 2], kb, kb == qb)
        if kb == qb:
            finish_q(qb)


def _attention(q, km, ke, vt, batch, s_len, hb, combos, name):
    n_pairs = km.shape[1] // LANE
    ow = len(combos) * LANE
    cols = hb * ATTN_T
    return pl.pallas_call(
        functools.partial(_attn_kernel, hb=hb, t=ATTN_T, s_len=s_len, combos=combos),
        grid=(batch, n_pairs),
        in_specs=[pl.BlockSpec((s_len, hb * SLAB), lambda b, p: (b, p)),
                  pl.BlockSpec((s_len, LANE), lambda b, p: (b, p)),
                  pl.BlockSpec((s_len, LANE), lambda b, p: (b, 0)),
                  pl.BlockSpec((LANE, s_len), lambda b, p: (b * n_pairs + p, 0))],
        out_specs=pl.BlockSpec((s_len, ow), lambda b, p: (b, p)),
        out_shape=jax.ShapeDtypeStruct((batch * s_len, n_pairs * ow), BF16),
        scratch_shapes=[pltpu.VMEM((SLAB, cols), BF16), pltpu.VMEM((SLAB, cols), BF16), pltpu.VMEM((ATTN_T, cols), F32),
                        pltpu.VMEM((ATTN_T, cols), F32), pltpu.VMEM((1, cols), F32),
                        pltpu.VMEM((LANE + ONES_ROWS, cols), F32)],
        compiler_params=pltpu.CompilerParams(dimension_semantics=("arbitrary", "arbitrary"),
                                             vmem_limit_bytes=VMEM_LIMIT),
        name=name,
    )(q, km, ke, vt)


def _memkv_kernel(mem_ref, gmem_ref, w_ref, gmk_ref, kt_ref, vt_ref):
    mn = (_rms(mem_ref[...], D_MODEL) * gmem_ref[...]).astype(BF16)
    kv = _dot(mn, w_ref[...])
    k = _group64_rms(kv[:, 0:MEM_W]) * gmk_ref[...]
    kt_ref[0] = k.T
    vt_ref[0] = kv[:, MEM_W:].T


def _memkv(mem2, batch, mem_len, w):
    return pl.pallas_call(
        _memkv_kernel,
        grid=(batch,),
        in_specs=[pl.BlockSpec((mem_len, D_MODEL), lambda b: (b, 0)), _const_spec(w['g_mem'].shape),
                  _const_spec(w['w_mkv'].shape), _const_spec(w['g_mk'].shape)],
        out_specs=[pl.BlockSpec((1, MEM_W, mem_len), lambda b: (b, 0, 0))] * 2,
        out_shape=[jax.ShapeDtypeStruct((batch, MEM_W, mem_len), F32)] * 2,
        compiler_params=pltpu.CompilerParams(dimension_semantics=("arbitrary",), vmem_limit_bytes=VMEM_LIMIT),
        name="memkv",
    )(mem2, w['g_mem'], w['w_mkv'], w['g_mk'])


def _finish_kernel(x_ref, om_ref, mk_ref, mv_ref, wo_ref, gcross_ref, wmq_ref, gmq_ref, wmo_ref, gffn_ref,
                   wup_ref, wdown_ref, y_ref, q_scr, o_scr, *, spt, ts):
    h = x_ref[...] + _dot(om_ref[...].astype(BF16), wo_ref[...])
    hn = (_rms(h, D_MODEL) * gcross_ref[...]).astype(BF16)
    q_scr[...] = _group64_rms(_dot(hn, wmq_ref[...])) * gmq_ref[...]
    head_of_lane = lax.broadcasted_iota(jnp.int32, (1, MEM_W), 1) // MEM_DIM

    def one_seq(s, _):
        r0 = pl.multiple_of(s * ts, ts)
        q = q_scr[pl.ds(r0, ts), :]
        qs = jnp.concatenate([jnp.where(head_of_lane == hd, q, 0.0) for hd in range(MEM_HEADS)], axis=0)
        sc = _dot(qs.astype(BF16), mk_ref[s].astype(BF16))
        p = jnp.exp(sc - jnp.max(sc, axis=1, keepdims=True))
        l = jnp.sum(p, axis=1, keepdims=True)
        pv = _dot_nt(p.astype(BF16), mv_ref[s].astype(BF16)) * (1.0 / l)
        o = jnp.where(head_of_lane == 0, pv[0:ts], 0.0)
        for hd in range(1, MEM_HEADS):
            o = o + jnp.where(head_of_lane == hd, pv[hd * ts:(hd + 1) * ts], 0.0)
        o_scr[pl.ds(r0, ts), :] = o
        return 0

    if spt == 1:
        one_seq(0, 0)
    else:
        lax.fori_loop(0, spt, one_seq, 0)

    h = h + _dot(o_scr[...].astype(BF16), wmo_ref[...])
    hn = (_rms(h, D_MODEL) * gffn_ref[...]).astype(BF16)
    y = h
    for c in range(D_FF // FFN_CHUNK):
        u = jnp.maximum(_dot(hn, wup_ref[:, c * FFN_CHUNK:(c + 1) * FFN_CHUNK]), 0.0)
        y = y + _dot((u * u).astype(BF16), wdown_ref[c * FFN_CHUNK:(c + 1) * FFN_CHUNK, :])
    y_ref[...] = y


def _finish(x2, om2, mk, mv, w_o, w, spt, ts, tiles_per_seq, name):
    assert spt == 1 or tiles_per_seq == 1
    n = x2.shape[0]
    tm = spt * ts
    mem_len = mk.shape[2]
    mem_spec = pl.BlockSpec((spt, MEM_W, mem_len), lambda i: (i // tiles_per_seq, 0, 0))
    consts = [w_o, w['g_cross'], w['w_mq'], w['g_mq'], w['w_mo'], w['g_ffn'], w['w_up'], w['w_down']]
    return pl.pallas_call(
        functools.partial(_finish_kernel, spt=spt, ts=ts),
        grid=(n // tm,),
        in_specs=[pl.BlockSpec((tm, D_MODEL), lambda i: (i, 0)), pl.BlockSpec((tm, D_MODEL), lambda i: (i, 0)),
                  mem_spec, mem_spec] + [_const_spec(c.shape) for c in consts],
        out_specs=pl.BlockSpec((tm, D_MODEL), lambda i: (i, 0)),
        out_shape=jax.ShapeDtypeStruct((n, D_MODEL), F32),
        scratch_shapes=[pltpu.VMEM((tm, MEM_W), F32), pltpu.VMEM((tm, MEM_W), F32)],
        compiler_params=pltpu.CompilerParams(dimension_semantics=("arbitrary",), vmem_limit_bytes=VMEM_LIMIT),
        name=name,
    )(x2, om2, mk, mv, *consts)


def _decode_kernel(pt_ref, qm_ref, fq_ref, cum_ref, locm_ref, lock_ref, locv_ref, locf_ref,
                   wkt_ref, wv_ref, gkn_ref, u_ref, linc_ref, cm_hbm, ck_hbm, cv_hbm, cl_hbm, o_ref,
                   mla_buf, fk_buf, fv_buf, lf_buf, sems, lhs_s, qr_s, qbd_s, carry_s, m1, l1, ctx, m2, l2, acc2,
                   *pend, nj, n_pages, n_seq):
    b = pl.program_id(0)
    j = pl.program_id(1)
    step = b * nj + j
    nrow = MLA_HEADS * DEC_T
    pend_a, pend_b = pend[0:4], pend[4:8]

    def page_copies(seq, jj, to_slot):
        base = seq * n_pages + (nj - 1 - jj) * PPS
        copies = []
        for i in range(PPS):
            page = pt_ref[base + i]
            for k, (src, dst) in enumerate(((cm_hbm, mla_buf), (ck_hbm, fk_buf), (cv_hbm, fv_buf), (cl_hbm, lf_buf))):
                copies.append(pltpu.make_async_copy(src.at[page], dst.at[to_slot, i], sems.at[to_slot, k]))
        return copies

    @pl.when(step == 0)
    def _():
        for c in page_copies(0, 0, 0):
            c.start()

    @pl.when(step + 1 < n_seq * nj)
    def _():
        wrap = j == nj - 1
        for c in page_copies(jnp.where(wrap, b + 1, b), jnp.where(wrap, 0, j + 1), (step + 1) % 2):
            c.start()

    for c in page_copies(b, j, step % 2):
        c.wait()
    cum8 = cum_ref[0]

    def mla_scores(lat_t, rope_t):
        big = _dot(lhs_s[...], lat_t)
        rs = _dot(qr_s[...], rope_t)
        slabs = []
        for h in range(MLA_HEADS):
            kvh = big[h * NOPE:(h + 1) * NOPE]
            r = lax.rsqrt(jnp.sum(kvh * kvh, axis=0, keepdims=True) * (1.0 / NOPE) + EPS)
            base = MLA_HEADS * NOPE + h * DEC_T
            slabs.append(big[base:base + DEC_T] * r + rs[h * DEC_T:(h + 1) * DEC_T])
        return jnp.concatenate(slabs, axis=0)

    def fox_scores(k_t, bias8):
        s = _dot(qbd_s[...], k_t)
        slabs = []
        for h in range(FOX_HEADS):
            slabs.append(s[h * DEC_T:(h + 1) * DEC_T] + bias8[h:h + 1, :] + cum8[:, h:h + 1])
        return jnp.concatenate(slabs, axis=0)

    def update(s, m_ref, l_ref, acc_ref, vals_t):
        m_old = m_ref[...]
        m_new = jnp.maximum(m_old, jnp.max(s, axis=1, keepdims=True))
        alpha = jnp.exp(m_old - m_new)
        p = jnp.exp(s - m_new)
        l_ref[...] = alpha * l_ref[...] + jnp.sum(p, axis=1, keepdims=True)
        m_ref[...] = m_new
        acc_ref[...] = alpha * acc_ref[...] + _dot_nt(p.astype(BF16), vals_t)

    @pl.when(j == 0)
    def _():
        qm8 = qm_ref[0].astype(F32)
        qn = jnp.concatenate([qm8[:, (2 * p) * SLAB:(2 * p) * SLAB + LANE]
                              + qm8[:, (2 * p + 1) * SLAB:(2 * p + 1) * SLAB + LANE] for p in range(4)], axis=1)
        qn = qn * gkn_ref[...]
        head512 = lax.broadcasted_iota(jnp.int32, (1, MLA_HEADS * NOPE), 1) // NOPE
        qn_bd = jnp.concatenate([jnp.where(head512 == h, qn, 0.0) for h in range(MLA_HEADS)], axis=0)
        lhs_s[0:MLA_HEADS * NOPE, :] = wkt_ref[...]
        lhs_s[MLA_HEADS * NOPE:, :] = _dot(qn_bd.astype(BF16), wkt_ref[...]).astype(BF16)
        qr_s[...] = jnp.concatenate([qm8[:, h * SLAB + LANE:h * SLAB + LANE + ROPE]
                                     for h in range(MLA_HEADS)], axis=0).astype(BF16)
        fq8 = fq_ref[0].astype(F32)
        zero = jnp.zeros((DEC_T, LANE), F32)
        blocks = []
        for h in range(FOX_HEADS):
            slab = fq8[:, h * SLAB:h * SLAB + LANE]
            blocks.append(jnp.concatenate([slab, zero] if (h // 2) // 2 == 0 else [zero, slab], axis=1))
        qbd_s[...] = jnp.concatenate(blocks, axis=0).astype(BF16)
        carry_s[...] = jnp.zeros_like(carry_s)
        m1[...] = jnp.full(m1.shape, NEG, F32)
        m2[...] = jnp.full(m2.shape, NEG, F32)
        l1[...] = jnp.zeros_like(l1)
        l2[...] = jnp.zeros_like(l2)
        ctx[...] = jnp.zeros_like(ctx)
        acc2[...] = jnp.zeros_like(acc2)

        tok = lax.broadcasted_iota(jnp.int32, (nrow, LANE), 0) % DEC_T
        valid = lax.broadcasted_iota(jnp.int32, (nrow, LANE), 1) <= tok
        lat_t = locm_ref[0, 0:KV_LORA, :].astype(BF16)
        s1 = mla_scores(lat_t, locm_ref[0, KV_LORA:MLA_ROW, :].astype(BF16))
        cum_t = _dot3(locf_ref[0], linc_ref[...])
        s2 = fox_scores(lock_ref[0].astype(BF16), -cum_t)
        s1_p, s2_p, lat_p, v_p = pend_b
        s1_p[...] = jnp.full(s1_p.shape, NEG, F32)
        s2_p[...] = jnp.full(s2_p.shape, NEG, F32)
        lat_p[...] = jnp.zeros_like(lat_p)
        v_p[...] = jnp.zeros_like(v_p)
        s1_p[:, 0:LANE] = jnp.where(valid, s1, NEG)
        s2_p[:, 0:LANE] = jnp.where(valid, s2, NEG)
        lat_p[:, 0:LANE] = lat_t
        v_p[:, 0:LANE] = locv_ref[0].astype(BF16)

    def score_phase(sl, queue):
        s1_p, s2_p, lat_p, v_p = queue
        lat_t = jnp.concatenate([mla_buf[sl, i, 0:KV_LORA, :].astype(BF16) for i in range(PPS)], axis=1)
        rope_t = jnp.concatenate([mla_buf[sl, i, KV_LORA:MLA_ROW, :].astype(BF16) for i in range(PPS)], axis=1)
        s1_p[...] = mla_scores(lat_t, rope_t)
        lat_p[...] = lat_t
        lfs = jnp.concatenate([lf_buf[sl, i] for i in range(PPS)], axis=0)
        res = _dot3(lfs, u_ref[...])
        carry = carry_s[...]
        biases = [None] * PPS
        for i in reversed(range(PPS)):
            biases[i] = res[i * 8:(i + 1) * 8, 0:LANE] + carry
            carry = carry + res[i * 8:(i + 1) * 8, LANE:2 * LANE]
        carry_s[...] = carry
        k_t = jnp.concatenate([fk_buf[sl, i].astype(BF16) for i in range(PPS)], axis=1)
        s2_p[...] = fox_scores(k_t, jnp.concatenate(biases, axis=1))
        v_p[...] = jnp.concatenate([fv_buf[sl, i].astype(BF16) for i in range(PPS)], axis=1)

    def value_phase(queue):
        s1_p, s2_p, lat_p, v_p = queue
        update(s1_p[...], m1, l1, ctx, lat_p[...])
        update(s2_p[...], m2, l2, acc2, v_p[...])

    @pl.when(j % 2 == 0)
    def _():
        score_phase(0, pend_a)
        value_phase(pend_b)

    @pl.when(j % 2 == 1)
    def _():
        score_phase(1, pend_b)
        value_phase(pend_a)

    @pl.when(j == nj - 1)
    def _():
        value_phase(pend_b)
        full = _dot((ctx[...] * (1.0 / l1[...])).astype(BF16), wv_ref[...])
        head512 = lax.broadcasted_iota(jnp.int32, (1, MLA_HEADS * MLA_V), 1) // MLA_V
        o_mla = jnp.where(head512 == 0, full[0:DEC_T], 0.0)
        for h in range(1, MLA_HEADS):
            o_mla = o_mla + jnp.where(head512 == h, full[h * DEC_T:(h + 1) * DEC_T], 0.0)
        a2 = acc2[...] * (1.0 / l2[...])
        head256 = lax.broadcasted_iota(jnp.int32, (1, FOX_KV * FOX_DIM), 1) // FOX_DIM
        outs = []
        for grp in range(FOX_HEADS // FOX_KV):
            o = None
            for kvh in range(FOX_KV):
                r0 = (kvh * 2 + grp) * DEC_T
                term = jnp.where(head256 == kvh, a2[r0:r0 + DEC_T], 0.0)
                o = term if o is None else o + term
            outs.append(o)
        o_ref[0] = jnp.concatenate([o_mla] + outs, axis=1)


def _decode(page_table_flat, qm, fq, cum, locm, lock, locv, locf, cm, ck, cv, cl, w, n_seq, n_pages):
    nj = n_pages // PPS
    assert nj % 2 == 0, "buffer slots alternate with the step parity within a sequence"
    keys = PPS * LANE

    seq = lambda rows, width: pl.BlockSpec((1, rows, width), lambda b, j, pt: (b, 0, 0))
    cspec = lambda a: pl.BlockSpec(a.shape, lambda b, j, pt: (0,) * a.ndim)
    consts = [w['w_kt'], w['w_v'], w['g_kn512'], w['u_mat'], w['l_inc']]
    nrow = MLA_HEADS * DEC_T
    grid_spec = pltpu.PrefetchScalarGridSpec(
        num_scalar_prefetch=1,
        grid=(n_seq, nj),
        in_specs=[seq(DEC_T, MLA_HEADS * SLAB), seq(DEC_T, FOX_HEADS * SLAB), seq(DEC_T, LANE),
                  seq(MLA_ROW, LANE), seq(FOX_KV * FOX_DIM, LANE), seq(FOX_KV * FOX_DIM, LANE), seq(FOX_HEADS, LANE)]
        + [cspec(c) for c in consts] + [pl.BlockSpec(memory_space=pl.ANY)] * 4,
        out_specs=pl.BlockSpec((1, DEC_T, D_MODEL), lambda b, j, pt: (b, 0, 0)),
        scratch_shapes=[pltpu.VMEM((2, PPS, MLA_ROW, LANE), F32), pltpu.VMEM((2, PPS, FOX_KV * FOX_DIM, LANE), F32),
                        pltpu.VMEM((2, PPS, FOX_KV * FOX_DIM, LANE), F32), pltpu.VMEM((2, PPS, FOX_HEADS, LANE), F32),
                        pltpu.SemaphoreType.DMA((2, 4)),
                        pltpu.VMEM((MLA_HEADS * NOPE + nrow, KV_LORA), BF16), pltpu.VMEM((nrow, ROPE), BF16),
                        pltpu.VMEM((nrow, FOX_KV * FOX_DIM), BF16), pltpu.VMEM((FOX_HEADS, LANE), F32),
                        pltpu.VMEM((nrow, 1), F32), pltpu.VMEM((nrow, 1), F32), pltpu.VMEM((nrow, KV_LORA), F32),
                        pltpu.VMEM((nrow, 1), F32), pltpu.VMEM((nrow, 1), F32),
                        pltpu.VMEM((nrow, FOX_KV * FOX_DIM), F32)]
        + [pltpu.VMEM((nrow, keys), F32), pltpu.VMEM((nrow, keys), F32), pltpu.VMEM((KV_LORA, keys), BF16),
           pltpu.VMEM((FOX_KV * FOX_DIM, keys), BF16)] * 2)
    return pl.pallas_call(
        functools.partial(_decode_kernel, nj=nj, n_pages=n_pages, n_seq=n_seq),
        grid_spec=grid_spec,
        out_shape=jax.ShapeDtypeStruct((n_seq, DEC_T, D_MODEL), F32),
        compiler_params=pltpu.CompilerParams(dimension_semantics=("arbitrary", "arbitrary"),
                                             vmem_limit_bytes=VMEM_LIMIT),
        name="decode",
    )(page_table_flat, qm, fq, cum, locm, lock, locv, locf, *consts, cm, ck, cv, cl)


def _place(vec, offset, width=LANE):
    return jnp.zeros((width,), F32).at[offset:offset + vec.shape[0]].set(vec)


def _rot_half(v, axis=-1):
    a, b = jnp.split(v, 2, axis=axis)
    return jnp.concatenate([-b, a], axis=axis)


def _prep_weights(g_mix, w_in, g_q_lat, w_q_up, g_kv_lat, g_k_rope, g_q_nope, g_q_rope, g_k_nope, w_kv_up, g_fox_q,
                  g_fox_k, b_forget, w_o, g_cross, g_mem, w_mq, w_mk, w_mv, g_mq, g_mk, w_mo, g_ffn, w_up, w_down):
    w = {}
    win = w_in[0]
    seg = [win[:, IN_SPLITS[k]:IN_SPLITS[k + 1]] for k in range(7)]
    c_q, c_kv, k_r, f_q, f_k, f_v, f_l = seg
    zc = lambda n: jnp.zeros((D_MODEL, n), F32)
    cols = [c_q, c_kv, k_r, zc(LANE - ROPE), _rot_half(k_r), zc(LANE - ROPE)]
    for h in range(FOX_HEADS):
        off = ((h // 2) % 2) * FOX_DIM
        cols += [zc(off), f_q[:, h * FOX_DIM:(h + 1) * FOX_DIM], zc(LANE - FOX_DIM - off)]
    cols += [f_k, f_v, f_l, zc(LANE - FOX_HEADS)]
    w['w_in'] = jnp.concatenate([c for c in cols if c.shape[1]], axis=1).astype(BF16)

    wq = w_q_up[0]
    zq = lambda n: jnp.zeros((Q_LORA, n), F32)
    qcols, rcols = [], []
    for h in range(MLA_HEADS):
        base = h * (NOPE + ROPE)
        off = (h % 2) * NOPE
        rope_w = wq[:, base + NOPE:base + NOPE + ROPE]
        qcols += [zq(off), wq[:, base:base + NOPE], zq(LANE - NOPE - off), rope_w, zq(LANE - ROPE)]
        rcols += [_rot_half(rope_w), zq(LANE - ROPE)]
    w['w_q'] = jnp.concatenate([c for c in qcols + rcols if c.shape[1]], axis=1).astype(BF16)

    wkv = w_kv_up[0].reshape(KV_LORA, MLA_HEADS, NOPE + MLA_V)
    w_k = wkv[:, :, :NOPE].reshape(KV_LORA, MLA_HEADS * NOPE)
    w_v = wkv[:, :, NOPE:].reshape(KV_LORA, MLA_HEADS * MLA_V)
    w['w_kv'] = jnp.concatenate([w_k, w_v], axis=1).astype(BF16)
    w['w_kt'] = w_k.T.astype(BF16)
    w['w_v'] = w_v.astype(BF16)
    w['g_kn512'] = jnp.tile(g_k_nope[0], MLA_HEADS)[None, :]

    gqr, gkr = g_q_rope[0] * MLA_SCALE, g_k_rope[0]
    swap = lambda v: jnp.concatenate([v[HALF:], v[:HALF]])
    k_aug_base = jnp.zeros((LANE,), F32).at[0:3].set(1.0)
    rows = [_place(g_q_nope[0] * MLA_SCALE, 0), _place(g_q_nope[0] * MLA_SCALE, NOPE), _place(gqr, 0),
            _place(swap(gqr), 0), _place(gkr, 0), _place(swap(gkr), 0), jnp.tile(g_k_nope[0], 2),
            _place(g_fox_q[0] * FOX_SCALE, 0), _place(g_fox_q[0] * FOX_SCALE, FOX_DIM), jnp.tile(g_fox_k[0], 2),
            _place(b_forget[0], 0), k_aug_base]
    rows += [jnp.zeros((LANE,), F32)] * (16 - len(rows))
    w['g128'] = jnp.stack(rows)

    place = np.zeros((3, LANE, LANE + FOX_HEADS * LANE), np.float32)
    base_q = np.zeros((1, FOX_HEADS * LANE), np.float32)
    for piece in range(3):
        for h in range(FOX_HEADS):
            place[piece, h, 3 + 3 * h + piece] = -1.0
            place[piece, h, LANE + h * LANE + piece] = 1.0
            base_q[0, h * LANE + 3 + 3 * h + piece] = 1.0
    w['place'] = jnp.asarray(place.reshape(3 * LANE, -1), BF16)
    w['base_q'] = jnp.asarray(base_q)

    w['g_mix'], w['g_q_lat'], w['g_kv_lat'] = g_mix, g_q_lat, g_kv_lat
    w['g_cross'], w['g_mem'], w['g_ffn'] = g_cross, g_mem, g_ffn
    w['g_mq'] = jnp.tile(g_mq[0] * MEM_SCALE, MEM_HEADS)[None, :]
    w['g_mk'] = jnp.tile(g_mk[0], MEM_HEADS)[None, :]
    w['w_mkv'] = jnp.concatenate([w_mk[0], w_mv[0]], axis=1).astype(BF16)
    w['w_mq'] = w_mq[0].astype(BF16)
    w['w_mo'] = w_mo[0].astype(BF16)
    w['w_up'] = w_up[0].astype(BF16)
    w['w_down'] = w_down[0].astype(BF16)

    wo = w_o[0]
    mla_w = MLA_HEADS * MLA_V
    fox_rows = wo[mla_w:].reshape(FOX_KV // 2, 2, 2, FOX_DIM, D_MODEL)
    prompt_fox = jnp.transpose(fox_rows, (0, 2, 1, 3, 4)).reshape(FOX_HEADS * FOX_DIM, D_MODEL)
    w['w_o'] = jnp.concatenate([wo[:mla_w], prompt_fox], axis=0).astype(BF16)
    fox_rows = wo[mla_w:].reshape(FOX_KV, 2, FOX_DIM, D_MODEL)
    sample_fox = jnp.transpose(fox_rows, (1, 0, 2, 3)).reshape(FOX_HEADS * FOX_DIM, D_MODEL)
    w['w_o_sample'] = jnp.concatenate([wo[:mla_w], sample_fox], axis=0).astype(BF16)

    idx = np.arange(LANE)
    u_strict = (idx[:, None] > idx[None, :]).astype(np.float32)
    w['u_mat'] = jnp.asarray(np.concatenate([u_strict, np.ones((LANE, LANE), np.float32)], axis=1), BF16)
    w['l_inc'] = jnp.asarray((idx[:, None] <= idx[None, :]).astype(np.float32), BF16)
    return w


def _rope_tables(pos):
    inv_freq = ROPE_BASE ** (-jnp.arange(HALF, dtype=F32) / HALF)
    ang = pos.astype(F32)[:, None] * inv_freq[None, :]
    pad = jnp.zeros((pos.shape[0], LANE - ROPE), F32)
    cos = jnp.concatenate([jnp.cos(ang), jnp.cos(ang), pad], axis=1)
    sin = jnp.concatenate([jnp.sin(ang), jnp.sin(ang), pad], axis=1)
    return cos, sin


def _tri(tm, group):
    idx = np.arange(tm)
    m = (idx[None, :] <= idx[:, None]) & (idx[None, :] // group == idx[:, None] // group)
    return jnp.asarray(m.astype(np.float32), BF16)


def kernel(x_prompt, x_sample, cache_mla, cache_fox_k, cache_fox_v, cache_fox_logf, cache_mem_k, cache_mem_v,
           page_table, mem_prompt, g_mix, w_in, g_q_lat, w_q_up, g_kv_lat, g_k_rope, g_q_nope, g_q_rope, g_k_nope,
           w_kv_up, g_fox_q, g_fox_k, b_forget, w_o, g_cross, g_mem, w_mq, w_mk, w_mv, g_mq, g_mk, w_mo, g_ffn, w_up,
           w_down):
    assert cache_mla.shape[0] == 1, "single-layer kernel"
    batch, s_len, _ = x_prompt.shape
    n_seq, t_new, _ = x_sample.shape
    n_pool, page = cache_mla.shape[1], cache_mla.shape[2]
    n_pages = page_table.shape[1]
    mem_len = mem_prompt.shape[1]
    assert page == LANE and t_new <= DEC_T and n_pages % PPS == 0
    assert s_len % PROJ_TM == 0 and s_len % ATTN_T == 0 and s_len % FIN_TM == 0
    w = _prep_weights(g_mix, w_in, g_q_lat, w_q_up, g_kv_lat, g_k_rope, g_q_nope, g_q_rope, g_k_nope, w_kv_up,
                      g_fox_q, g_fox_k, b_forget, w_o, g_cross, g_mem, w_mq, w_mk, w_mv, g_mq, g_mk, w_mo, g_ffn,
                      w_up, w_down)

    n = batch * s_len
    xp = x_prompt.reshape(n, D_MODEL)
    cos_p, sin_p = _rope_tables(jnp.arange(s_len))
    (p_row_t, qm, kn, vm_t, kr, fq, fk_t, fkb, fv_t, fvb_t, p_lf_t, _, kaug) = _proj(
        xp, cos_p, sin_p, _tri(PROJ_TM, PROJ_TM), w, s_len // PROJ_TM)
    o_mla = _attention(qm, kn, kr, vm_t, batch, s_len, 2, ((0, 1),), "attn_mla")
    o_fox = _attention(fq, fkb, kaug, fvb_t, batch, s_len, 4, ((0, 2), (1, 3)), "attn_fox")
    mk_t, mv_t = _memkv(mem_prompt.reshape(batch * mem_len, D_MODEL), batch, mem_len, w)
    y_prompt = _finish(xp, jnp.concatenate([o_mla, o_fox], axis=1), mk_t, mv_t, w['w_o'], w, 1, FIN_TM,
                       s_len // FIN_TM, "finish_prompt")

    ns = n_seq * DEC_T
    xs = jnp.pad(x_sample, ((0, 0), (0, DEC_T - t_new), (0, 0))).reshape(ns, D_MODEL)
    pos_s = n_pages * page + (jnp.arange(PROJ_TM) % DEC_T)
    cos_s, sin_s = _rope_tables(pos_s)
    (s_row_t, qm_s, _, _, _, fq_s, fk_s_t, _, fv_s_t, _, s_lf_t, cum_s, _) = _proj(
        xs, cos_s, sin_s, _tri(PROJ_TM, DEC_T), w, 1)

    def new_tokens(a_t):
        feat = a_t.shape[0] // (ns // PROJ_TM)
        a = jnp.transpose(a_t.reshape(ns // PROJ_TM, feat, PROJ_TM), (1, 0, 2))
        return a.reshape(feat, n_seq, DEC_T)[:, :, :t_new]

    def new_page(a_t):
        return jnp.pad(jnp.transpose(new_tokens(a_t), (1, 0, 2)), ((0, 0), (0, 0), (0, LANE - t_new)))

    cm = jnp.transpose(cache_mla[0], (0, 2, 1))
    ck = jnp.transpose(cache_fox_k[0], (0, 2, 3, 1)).reshape(n_pool, FOX_KV * FOX_DIM, page)
    cv = jnp.transpose(cache_fox_v[0], (0, 2, 3, 1)).reshape(n_pool, FOX_KV * FOX_DIM, page)
    cl = jnp.transpose(cache_fox_logf[0], (0, 2, 1))
    o_s = _decode(page_table.reshape(-1), qm_s.reshape(n_seq, DEC_T, -1), fq_s.reshape(n_seq, DEC_T, -1),
                  cum_s.reshape(n_seq, DEC_T, LANE), new_page(s_row_t), new_page(fk_s_t), new_page(fv_s_t),
                  new_page(s_lf_t), cm, ck, cv, cl, w, n_seq, n_pages)
    cmk = jnp.transpose(cache_mem_k[0], (0, 2, 3, 1)).reshape(n_seq, MEM_W, mem_len)
    cmv = jnp.transpose(cache_mem_v[0], (0, 2, 3, 1)).reshape(n_seq, MEM_W, mem_len)
    y_s = _finish(xs, o_s.reshape(ns, D_MODEL), cmk, cmv, w['w_o_sample'], w, 16, DEC_T, 1, "finish_sample")

    def tok(a_t, *shape):
        return jnp.transpose(new_tokens(a_t), (1, 2, 0)).reshape(1, n_seq, t_new, *shape)

    def seq_out(a_t, *shape):
        return jnp.transpose(a_t.reshape(batch, -1, a_t.shape[1]), (0, 2, 1)).reshape(1, batch, a_t.shape[1], *shape)

    y_sample = y_s.reshape(n_seq, DEC_T, D_MODEL)[:, :t_new]
    return (y_prompt.reshape(batch, s_len, D_MODEL), y_sample,
            seq_out(p_row_t, MLA_ROW), seq_out(fk_t, FOX_KV, FOX_DIM), seq_out(fv_t, FOX_KV, FOX_DIM),
            seq_out(p_lf_t, FOX_HEADS), seq_out(mk_t.reshape(batch * MEM_W, mem_len), MEM_HEADS, MEM_DIM),
            seq_out(mv_t.reshape(batch * MEM_W, mem_len), MEM_HEADS, MEM_DIM),
            tok(s_row_t, MLA_ROW), tok(fk_s_t, FOX_KV, FOX_DIM), tok(fv_s_t, FOX_KV, FOX_DIM),
            tok(s_lf_t, FOX_HEADS))
```
